```python
import math
import jax, jax.numpy as jnp
from jax import lax
import numpy as np

D_MODEL = 1024
BATCH = 2
SEQ = 8192
DEPTH = 1

CHUNK = 64
Q_BLOCK = 128
EPS = 1e-6

DIFF_HEADS = 8
DIFF_DH = 64
DIFF_W = DIFF_HEADS * 2 * DIFF_DH

CH_HEADS = 8
CH_DH = 64
CH_W = CH_HEADS * CH_DH
CH_LEFT = 8
REL_CLIP = 128

PEER_HEADS = 8
PEER_NKEYS = 128
PEER_NEXP = PEER_NKEYS * PEER_NKEYS
PEER_DKEY = 128
PEER_TOPK = 16
PEER_BLOCK = 128

IN_SIZES = (DIFF_W, DIFF_W, DIFF_W, CH_W, CH_W, CH_W, D_MODEL, D_MODEL)
IN_COLS = sum(IN_SIZES)
IN_SPLITS = tuple(int(s) for s in np.cumsum(IN_SIZES)[:-1])

kernel_name = "hybrid_diffattn_chunkattn_peer"


def rmsnorm(x, w):
    xf = x.astype(jnp.float32)
    y = xf * lax.rsqrt(jnp.mean(xf * xf, axis=-1, keepdims=True) + EPS)
    return (y * w.astype(jnp.float32)).astype(x.dtype)


def alibi_slopes(n_heads):
    return jnp.exp2(-8.0 * jnp.arange(1, n_heads + 1, dtype=jnp.float32) / n_heads)


def diff_attention(q, k, v, lq1, lk1, lq2, lk2, subln_w, lam_init):
    B, S = q.shape[0], q.shape[1]
    nqb = S // Q_BLOCK
    f32 = jnp.float32
    lam = (jnp.exp(jnp.sum(lq1.astype(f32) * lk1.astype(f32)))
           - jnp.exp(jnp.sum(lq2.astype(f32) * lk2.astype(f32))) + lam_init)
    k1, k2 = k[..., 0, :], k[..., 1, :]
    qb = jnp.moveaxis(q.reshape(B, nqb, Q_BLOCK, DIFF_HEADS, 2, DIFF_DH), 1, 0)
    slopes = alibi_slopes(DIFF_HEADS)
    kpos = jnp.arange(S, dtype=jnp.int32)
    scale = DIFF_DH ** -0.5

    def block(args):
        qblk, bi = args
        qpos = bi * Q_BLOCK + jnp.arange(Q_BLOCK, dtype=jnp.int32)
        dist = jnp.abs(qpos[:, None] - kpos[None, :]).astype(f32)
        bias = -slopes[:, None, None] * dist
        allowed = (kpos[None, :] // CHUNK) <= (qpos[:, None] // CHUNK)

        def probs(qq, kk):
            s = jnp.einsum('bqhd,bkhd->bhqk', qq, kk).astype(f32) * scale + bias
            return jax.nn.softmax(jnp.where(allowed, s, -jnp.inf), axis=-1)

        a = probs(qblk[..., 0, :], k1) - lam * probs(qblk[..., 1, :], k2)
        return jnp.einsum('bhqk,bkhe->bqhe', a.astype(v.dtype), v)

    o = lax.map(block, (qb, jnp.arange(nqb, dtype=jnp.int32)))
    o = jnp.moveaxis(o, 0, 1).reshape(B, S, DIFF_HEADS, 2 * DIFF_DH)
    o = rmsnorm(o, subln_w) * (1.0 - lam_init)
    return o.reshape(B, S, DIFF_W)


def chunk_attention(q, k, v, rel_table):
    B, S = q.shape[0], q.shape[1]
    nc = S // CHUNK
    band = (CH_LEFT + 1) * CHUNK
    f32 = jnp.float32
    qc = q.reshape(B, nc, CHUNK, CH_HEADS, CH_DH)
    pad = ((0, 0), (CH_LEFT * CHUNK, 0), (0, 0), (0, 0))
    kp = jnp.pad(k, pad).reshape(B, nc + CH_LEFT, CHUNK, CH_HEADS, CH_DH)
    vp = jnp.pad(v, pad).reshape(B, nc + CH_LEFT, CHUNK, CH_HEADS, CH_DH)
    kb = jnp.concatenate([kp[:, j:j + nc] for j in range(CH_LEFT + 1)], axis=2)
    vb = jnp.concatenate([vp[:, j:j + nc] for j in range(CH_LEFT + 1)], axis=2)
    r = jnp.arange(band, dtype=jnp.int32)
    qi = jnp.arange(CHUNK, dtype=jnp.int32)
    rel = qi[:, None] - r[None, :] + CH_LEFT * CHUNK
    bias = rel_table[:, jnp.clip(rel, -REL_CLIP, REL_CLIP) + REL_CLIP].astype(f32)
    valid = (jnp.arange(nc, dtype=jnp.int32)[:, None] * CHUNK - CH_LEFT * CHUNK + r[None, :]) >= 0
    s = jnp.einsum('bcqhd,bckhd->bhcqk', qc, kb).astype(f32) * (CH_DH ** -0.5) + bias[:, None]
    s = jnp.where(valid[:, None, :], s, -jnp.inf)
    p = jax.nn.softmax(s, axis=-1)
    o = jnp.einsum('bhcqk,bckhd->bcqhd', p.astype(v.dtype), vb)
    return o.reshape(B, S, CH_W)


def peer(xn, wq, keys, u, v):
    B, S, D = xn.shape
    xt = xn.reshape(-1, PEER_BLOCK, D)
    f32 = jnp.float32

    def block(xb):
        q = (xb @ wq).reshape(PEER_BLOCK, PEER_HEADS, 2, PEER_DKEY // 2)
        s1 = jnp.einsum('thd,hnd->thn', q[:, :, 0], keys[:, 0]).astype(f32)
        s2 = jnp.einsum('thd,hnd->thn', q[:, :, 1], keys[:, 1]).astype(f32)
        v1, i1 = lax.top_k(s1, PEER_TOPK)
        v2, i2 = lax.top_k(s2, PEER_TOPK)
        cand = (v1[..., :, None] + v2[..., None, :]).reshape(PEER_BLOCK, PEER_HEADS, PEER_TOPK * PEER_TOPK)
        sc, ci = lax.top_k(cand, PEER_TOPK)
        e = (jnp.take_along_axis(i1, ci // PEER_TOPK, axis=-1) * PEER_NKEYS
             + jnp.take_along_axis(i2, ci % PEER_TOPK, axis=-1))
        g = jax.nn.softmax(sc, axis=-1)
        ue = u[e]
        ve = v[e]
        hid = jnp.einsum('thkd,td->thk', ue, xb).astype(f32)
        a = (jax.nn.gelu(hid, approximate=False) * g).astype(xb.dtype)
        return jnp.einsum('thk,thkd->td', a, ve)

    return lax.map(block, xt).reshape(B, S, D)


def setup_inputs(seed: int = 0) -> dict:
    key = jax.random.key(seed)
    ks = jax.random.split(key, 20)
    n = jax.random.normal
    f32 = jnp.float32
    L, D = DEPTH, D_MODEL
    return {
        "x": n(ks[0], (BATCH, SEQ, D), f32),
        "norm1_w": 1.0 + 0.02 * n(ks[1], (L, D), f32),
        "w_in": n(ks[2], (L, D, IN_COLS), f32) * D ** -0.5,
        "b_gate": 0.01 * n(ks[3], (L, 2 * D), f32),
        "diff_lq1": 0.1 * n(ks[4], (L, DIFF_DH), f32),
        "diff_lk1": 0.1 * n(ks[5], (L, DIFF_DH), f32),
        "diff_lq2": 0.1 * n(ks[6], (L, DIFF_DH), f32),
        "diff_lk2": 0.1 * n(ks[7], (L, DIFF_DH), f32),
        "diff_subln_w": 1.0 + 0.02 * n(ks[8], (L, 2 * DIFF_DH), f32),
        "chunk_rel_bias": 0.1 * n(ks[9], (L, CH_HEADS, 2 * REL_CLIP + 1), f32),
        "w_branch_diff": n(ks[10], (L, DIFF_W, D), f32) * DIFF_W ** -0.5,
        "w_branch_chunk": n(ks[11], (L, CH_W, D), f32) * CH_W ** -0.5,
        "w_out": n(ks[12], (L, D, D), f32) * D ** -0.5,
        "norm2_w": 1.0 + 0.02 * n(ks[13], (L, D), f32),
        "peer_wq": n(ks[14], (L, D, PEER_HEADS * PEER_DKEY), f32) * D ** -0.5,
        "peer_keys": n(ks[15], (L, PEER_HEADS, 2, PEER_NKEYS, PEER_DKEY // 2), f32) * (PEER_DKEY // 2) ** -0.5,
        "peer_u": n(ks[16], (L, PEER_NEXP, D), f32) * D ** -0.5,
        "peer_v": n(ks[17], (L, PEER_NEXP, D), f32) * D ** -0.5,
        "final_norm_w": 1.0 + 0.02 * n(ks[18], (D,), f32),
    }


def reference(x, norm1_w, w_in, b_gate, diff_lq1, diff_lk1, diff_lq2, diff_lk2, diff_subln_w,
              chunk_rel_bias, w_branch_diff, w_branch_chunk, w_out, norm2_w, peer_wq, peer_keys,
              peer_u, peer_v, final_norm_w):
    B, S, D = x.shape
    h = x
    for l in range(DEPTH):
        lam_init = 0.8 - 0.6 * math.exp(-0.3 * l)
        xn = rmsnorm(h, norm1_w[l])
        proj = xn @ w_in[l]
        qd, kd, vd, qc, kc, vc, gd, gc = jnp.split(proj, IN_SPLITS, axis=-1)
        o_d = diff_attention(qd.reshape(B, S, DIFF_HEADS, 2, DIFF_DH),
                             kd.reshape(B, S, DIFF_HEADS, 2, DIFF_DH),
                             vd.reshape(B, S, DIFF_HEADS, 2 * DIFF_DH),
                             diff_lq1[l], diff_lk1[l], diff_lq2[l], diff_lk2[l], diff_subln_w[l], lam_init)
        o_c = chunk_attention(qc.reshape(B, S, CH_HEADS, CH_DH), kc.reshape(B, S, CH_HEADS, CH_DH),
                              vc.reshape(B, S, CH_HEADS, CH_DH), chunk_rel_bias[l])
        g_d = jax.nn.sigmoid(gd + b_gate[l, :D])
        g_c = jax.nn.sigmoid(gc + b_gate[l, D:])
        merged = g_d * (o_d @ w_branch_diff[l]) + g_c * (o_c @ w_branch_chunk[l])
        h = h + merged @ w_out[l]
        h = h + peer(rmsnorm(h, norm2_w[l]), peer_wq[l], peer_keys[l], peer_u[l], peer_v[l])
    return rmsnorm(h, final_norm_w)
```

```python
import functools
import math

import jax
import jax.numpy as jnp
import numpy as np
from jax import lax
from jax.experimental import pallas as pl
from jax.experimental.pallas import tpu as pltpu

F32 = jnp.float32
BF16 = jnp.bfloat16

EPS = 1e-6
CHUNK = 64
DIFF_HEADS = 8
DIFF_DH = 64
CH_HEADS = 8
CH_DH = 64
CH_LEFT = 8
REL_CLIP = 128
PEER_HEADS = 8
PEER_NKEYS = 128
PEER_TOPK = 16
NEG = -1e30

VMEM_LIMIT = 56 * 1024 * 1024

_NT = (((1,), (1,)), ((), ()))


def _params(*sem):
    return pltpu.CompilerParams(dimension_semantics=sem, vmem_limit_bytes=VMEM_LIMIT)


def _rms(x, w):
    return x * lax.rsqrt(jnp.mean(x * x, axis=-1, keepdims=True) + EPS) * w


_QKV_WIDTHS = (1024, 1024, 1024, 512, 512, 512)
_Q_SCALED = (True, False, False, True, False, False)


def _inproj_body(x_ref, nw_ref, w_ref, qd_ref, kd_ref, vd_ref, qc_ref, kc_ref, vc_ref, g_ref):
    xn = _rms(x_ref[...], nw_ref[...]).astype(BF16)
    col = 0
    for ref, width, scaled in zip((qd_ref, kd_ref, vd_ref, qc_ref, kc_ref, vc_ref), _QKV_WIDTHS, _Q_SCALED):
        acc = jnp.dot(xn, w_ref[:, col:col + width], preferred_element_type=F32)
        if scaled:
            acc = acc * (DIFF_DH ** -0.5)
        ref[...] = acc.astype(BF16)
        col += width
    g_ref[...] = jnp.dot(xn, w_ref[:, col:], preferred_element_type=F32)


def _in_proj(x2, norm_w, w_in_bf16, tm):
    n, d = x2.shape
    cols = w_in_bf16.shape[1]
    row = lambda width: pl.BlockSpec((tm, width), lambda i: (i, 0))
    const = lambda shape: pl.BlockSpec(shape, lambda i: (0, 0))
    out_shape = [jax.ShapeDtypeStruct((n, w), BF16) for w in _QKV_WIDTHS] + [jax.ShapeDtypeStruct((n, 2 * d), F32)]
    return pl.pallas_call(
        _inproj_body,
        grid=(n // tm,),
        in_specs=[row(d), const((1, d)), const((d, cols))],
        out_specs=[row(w) for w in _QKV_WIDTHS] + [row(2 * d)],
        out_shape=out_shape,
        compiler_params=_params("parallel"),
        name="in_proj",
    )(x2, norm_w.reshape(1, d), w_in_bf16)


def _diff_attn_body(q_ref, k_ref, v_ref, lam_ref, subw_ref, o_ref, m_sc, l_sc, acc_sc, *, tq, lam_init):
    h = pl.program_id(1)
    i = pl.program_id(2)
    q = q_ref[...]
    lane = lax.broadcasted_iota(jnp.int32, q.shape, 1)
    zero = jnp.zeros_like(q)
    qs = jnp.concatenate([jnp.where(lane < DIFF_DH, q, zero), jnp.where(lane >= DIFF_DH, q, zero)], axis=0)

    slope = lax.bitcast_convert_type(jnp.full((1, 1), 126 - h, jnp.int32) << 23, F32)
    r = lax.broadcasted_iota(jnp.int32, (tq, tq), 0)
    c = lax.broadcasted_iota(jnp.int32, (tq, tq), 1)
    dist = (r - c).astype(F32)
    off_bias = -slope * dist
    diag_bias = jnp.where((c // CHUNK) <= (r // CHUNK), -slope * jnp.abs(dist), NEG)
    off_bias = jnp.concatenate([off_bias, off_bias], axis=0)
    diag_bias = jnp.concatenate([diag_bias, diag_bias], axis=0)

    m_sc[...] = jnp.full(m_sc.shape, NEG, F32)
    l_sc[...] = jnp.zeros(l_sc.shape, F32)
    acc_sc[...] = jnp.zeros(acc_sc.shape, F32)

    def tile(start, bias, shift):
        kb = k_ref[pl.ds(start, tq), :]
        vb = v_ref[pl.ds(start, tq), :]
        s = lax.dot_general(qs, kb, _NT, preferred_element_type=F32) + bias
        m_old = m_sc[...]
        m_new = jnp.maximum(m_old, jnp.max(s, axis=-1, keepdims=True) + shift)
        alpha = jnp.exp(m_old - m_new)
        p = jnp.exp(s - (m_new - shift))
        l_sc[...] = alpha * l_sc[...] + jnp.sum(p, axis=-1, keepdims=True)
        acc_sc[...] = alpha * acc_sc[...] + jnp.dot(p.astype(BF16), vb, preferred_element_type=F32)
        m_sc[...] = m_new

    def body(j, carry):
        shift = -slope * ((i - j) * tq).astype(F32)
        tile(pl.multiple_of(j * tq, tq), off_bias, shift)
        return carry

    lax.fori_loop(0, i, body, 0)
    tile(pl.multiple_of(i * tq, tq), diag_bias, jnp.zeros((1, 1), F32))

    acc = acc_sc[...]
    l = l_sc[...]
    o1 = acc[:tq] / l[:tq]
    o2 = acc[tq:] / l[tq:]
    lv = lam_ref[...]
    lam = (jnp.exp(jnp.sum(lv[0:1] * lv[1:2], axis=-1, keepdims=True))
           - jnp.exp(jnp.sum(lv[2:3] * lv[3:4], axis=-1, keepdims=True)) + lam_init)
    o = o1 - lam * o2
    o_ref[...] = (_rms(o, subw_ref[...]) * (1.0 - lam_init)).astype(BF16)


def _diff_attn(qd, kd, vd, lam_vecs, subln_w, batch, seq, lam_init, tq):
    n = qd.shape[0]
    nq = seq // tq
    hw = 2 * DIFF_DH
    return pl.pallas_call(
        functools.partial(_diff_attn_body, tq=tq, lam_init=lam_init),
        grid=(batch, DIFF_HEADS, nq),
        in_specs=[
            pl.BlockSpec((tq, hw), lambda b, h, i: (b * nq + i, h)),
            pl.BlockSpec((seq, hw), lambda b, h, i: (b, h)),
            pl.BlockSpec((seq, hw), lambda b, h, i: (b, h)),
            pl.BlockSpec((4, DIFF_DH), lambda b, h, i: (0, 0)),
            pl.BlockSpec((1, hw), lambda b, h, i: (0, 0)),
        ],
        out_specs=pl.BlockSpec((tq, hw), lambda b, h, i: (b * nq + i, h)),
        out_shape=jax.ShapeDtypeStruct((n, DIFF_HEADS * hw), BF16),
        scratch_shapes=[pltpu.VMEM((2 * tq, 1), F32), pltpu.VMEM((2 * tq, 1), F32), pltpu.VMEM((2 * tq, hw), F32)],
        compiler_params=_params("parallel", "parallel", "arbitrary"),
        name="diff_attn",
    )(qd, kd, vd, lam_vecs, subln_w.reshape(1, hw))


_CT = 256
_CWIN = 3 * _CT
_FW = 1024


def _rel_row_index():
    lanes = np.arange(_FW)
    u = np.where(lanes < _CWIN, lanes, lanes - _FW)
    rel = CH_LEFT * CHUNK - u
    return np.clip(rel, -REL_CLIP, REL_CLIP) + REL_CLIP


def _chunk_attn_body(q_ref, k0_ref, k1_ref, k2_ref, v0_ref, v1_ref, v2_ref, f_ref, o_ref, bias_sc):
    i = pl.program_id(1)

    @pl.when((pl.program_id(0) == 0) & (i == 0))
    def _():
        r = lax.broadcasted_iota(jnp.int32, (_CT, _CWIN), 0) // CHUNK
        c = lax.broadcasted_iota(jnp.int32, (_CT, _CWIN), 1) // CHUNK
        allowed = (c >= r) & (c <= r + CH_LEFT)
        for h in range(CH_HEADS):
            row = jnp.broadcast_to(f_ref[h:h + 1, :], (_CT, _FW))
            toeplitz = pltpu.roll(row, 0, 1, stride=1, stride_axis=0)
            bias_sc[h] = jnp.where(allowed, toeplitz[:, :_CWIN], NEG)

    q = q_ref[...]
    ks = (k0_ref[...], k1_ref[...], k2_ref[...])
    vs = (v0_ref[...], v1_ref[...], v2_ref[...])
    pen = [jnp.where(i - 2 + d >= 0, 0.0, NEG).astype(F32) for d in range(2)] + [None]
    outs = []
    for h in range(CH_HEADS):
        sl = slice(h * CH_DH, (h + 1) * CH_DH)
        qh = q[:, sl]
        parts = []
        for d in range(3):
            sd = lax.dot_general(qh, ks[d][:, sl], _NT, preferred_element_type=F32)
            parts.append(sd if pen[d] is None else sd + pen[d])
        s = jnp.concatenate(parts, axis=1) + bias_sc[h]
        m = jnp.max(s, axis=-1, keepdims=True)
        p = jnp.exp(s - m)
        l = jnp.sum(p, axis=-1, keepdims=True)
        pb = p.astype(BF16)
        o = jnp.dot(pb[:, :_CT], vs[0][:, sl], preferred_element_type=F32)
        o += jnp.dot(pb[:, _CT:2 * _CT], vs[1][:, sl], preferred_element_type=F32)
        o += jnp.dot(pb[:, 2 * _CT:], vs[2][:, sl], preferred_element_type=F32)
        outs.append(o / l)
    o_ref[...] = jnp.concatenate(outs, axis=1).astype(BF16)


def _chunk_attn(qc, kc, vc, rel_rows, batch, seq):
    n, w = qc.shape
    nq = seq // _CT
    win = lambda d: pl.BlockSpec((_CT, w), lambda b, i: (b * nq + jnp.maximum(i - 2 + d, 0), 0))
    return pl.pallas_call(
        _chunk_attn_body,
        grid=(batch, nq),
        in_specs=[pl.BlockSpec((_CT, w), lambda b, i: (b * nq + i, 0)),
                  win(0), win(1), win(2), win(0), win(1), win(2),
                  pl.BlockSpec((CH_HEADS, _FW), lambda b, i: (0, 0))],
        out_specs=pl.BlockSpec((_CT, w), lambda b, i: (b * nq + i, 0)),
        out_shape=jax.ShapeDtypeStruct((n, w), BF16),
        scratch_shapes=[pltpu.VMEM((CH_HEADS, _CT, _CWIN), F32)],
        compiler_params=_params("arbitrary", "arbitrary"),
        name="chunk_attn",
    )(qc, kc, kc, kc, vc, vc, vc, rel_rows)


def _merge_body(od_ref, oc_ref, g_ref, x_ref, bg_ref, wbd_ref, wbc_ref, wo_ref, n2_ref, wq_ref,
                h_ref, xn_ref, qp_ref):
    d = x_ref.shape[1]
    g = g_ref[...] + bg_ref[...]
    pd = jnp.dot(od_ref[...], wbd_ref[...], preferred_element_type=F32)
    pc = jnp.dot(oc_ref[...], wbc_ref[...], preferred_element_type=F32)
    merged = jax.nn.sigmoid(g[:, :d]) * pd + jax.nn.sigmoid(g[:, d:]) * pc
    h1 = x_ref[...] + jnp.dot(merged.astype(BF16), wo_ref[...], preferred_element_type=F32)
    h_ref[...] = h1
    xn = _rms(h1, n2_ref[...]).astype(BF16)
    xn_ref[...] = xn
    qp_ref[...] = jnp.dot(xn, wq_ref[...], preferred_element_type=F32).astype(BF16)


def _merge(od, oc, gates, x2, b_gate, wbd, wbc, wo, norm2_w, wq, tm):
    n, d = x2.shape
    row = lambda width: pl.BlockSpec((tm, width), lambda i: (i, 0))
    const = lambda shape: pl.BlockSpec(shape, lambda i: (0, 0))
    return pl.pallas_call(
        _merge_body,
        grid=(n // tm,),
        in_specs=[row(od.shape[1]), row(oc.shape[1]), row(2 * d), row(d), const((1, 2 * d)),
                  const(wbd.shape), const(wbc.shape), const(wo.shape), const((1, d)), const(wq.shape)],
        out_specs=[row(d), row(d), row(wq.shape[1])],
        out_shape=[jax.ShapeDtypeStruct((n, d), F32), jax.ShapeDtypeStruct((n, d), BF16),
                   jax.ShapeDtypeStruct((n, wq.shape[1]), BF16)],
        compiler_params=_params("parallel"),
        name="merge",
    )(od, oc, gates, x2, b_gate.reshape(1, 2 * d), wbd, wbc, wo, norm2_w.reshape(1, d), wq)


def _pair_list():
    return [(a, b) for a in range(PEER_TOPK) for b in range(PEER_TOPK) if (a + 1) * (b + 1) <= PEER_TOPK]


def _extract_top(s, steps):
    rows = s.shape[0]
    idx = lax.broadcasted_iota(jnp.int32, s.shape, 0).astype(F32)
    rank = jnp.full(s.shape, float(steps), F32)
    vals = []
    for step in range(steps):
        m = jnp.max(s, axis=0, keepdims=True)
        first = jnp.min(jnp.where(s == m, idx, float(rows)), axis=0, keepdims=True)
        sel = idx == first
        rank = jnp.where(sel, float(step), rank)
        s = jnp.where(sel, -jnp.inf, s)
        vals.append(m)
    return vals, rank


def _route_body(qp_ref, kz_ref, r2_ref, bw_ref, ci_ref, ai_ref):
    nk = PEER_NKEYS
    pairs = _pair_list()
    for h in range(PEER_HEADS):
        qh = qp_ref[:, h * 2 * CH_DH:(h + 1) * 2 * CH_DH]
        st = lax.dot_general(kz_ref[h], qh, _NT, preferred_element_type=F32)
        s1, s2 = st[:nk], st[nk:]
        v1, rank1 = _extract_top(s1, PEER_TOPK)
        v2, rank2 = _extract_top(s2, PEER_TOPK)
        cand = jnp.concatenate([v1[a] + v2[b] for a, b in pairs], axis=0)
        top = cand[0:1]
        _, crank = _extract_top(cand, PEER_TOPK)
        chosen = crank < float(PEER_TOPK)
        z = jnp.sum(jnp.where(chosen, jnp.exp(cand - top), 0.0), axis=0, keepdims=True)
        ci = jnp.zeros(rank1.shape, F32)
        row = 0
        for a in range(PEER_TOPK):
            nb = PEER_TOPK // (a + 1)
            cnt = jnp.sum(jnp.where(chosen[row:row + nb], 1.0, 0.0), axis=0, keepdims=True)
            ci = ci + jnp.where(rank1 == float(a), cnt, 0.0)
            row += nb
        r2_ref[h] = rank2.astype(BF16)
        bw_ref[h] = (jnp.exp(s2 - v2[0]) / z).astype(BF16)
        ci_ref[h] = ci
        ai_ref[h] = jnp.exp(s1 - v1[0])


def _peer_route(qp, kz, tt):
    n, w = qp.shape
    spec = pl.BlockSpec((PEER_HEADS, PEER_NKEYS, tt), lambda t: (0, 0, t))
    shp = lambda dt: jax.ShapeDtypeStruct((PEER_HEADS, PEER_NKEYS, n), dt)
    return pl.pallas_call(
        _route_body,
        grid=(n // tt,),
        in_specs=[pl.BlockSpec((tt, w), lambda t: (t, 0)),
                  pl.BlockSpec(kz.shape, lambda t: (0, 0, 0))],
        out_specs=[spec, spec, spec, spec],
        out_shape=[shp(BF16), shp(BF16), shp(F32), shp(F32)],
        compiler_params=_params("parallel"),
        name="peer_route",
    )(qp, kz)


def _experts_body(xn_ref, u_ref, vt_ref, r2_ref, bw_ref, ci_ref, ai_ref, h_ref, fw_ref, y_ref, acc_sc, act_sc,
                  *, groups):
    c = pl.program_id(1)
    nk = PEER_NKEYS

    @pl.when(c == 0)
    def _():
        acc_sc[...] = jnp.zeros(acc_sc.shape, F32)

    hid = lax.dot_general(u_ref[...], xn_ref[...], _NT, preferred_element_type=F32)
    for gi in range(groups):
        i = c * groups + gi
        w = None
        for h in range(PEER_HEADS):
            ci = ci_ref[h, pl.ds(i, 1), :].astype(BF16)
            ai = ai_ref[h, pl.ds(i, 1), :].astype(BF16)
            term = jnp.where(r2_ref[h] < ci, bw_ref[h], jnp.zeros((), BF16)) * ai
            w = term if w is None else w + term
        hg = hid[gi * nk:(gi + 1) * nk]
        gelu = 0.5 * hg * (1.0 + lax.erf(hg * (2.0 ** -0.5)))
        act_sc[gi * nk:(gi + 1) * nk, :] = (gelu * w.astype(F32)).astype(BF16)
    acc_sc[...] += jnp.dot(vt_ref[...], act_sc[...], preferred_element_type=F32)

    @pl.when(c == pl.num_programs(1) - 1)
    def _():
        hfin = h_ref[...] + acc_sc[...].T
        y_ref[...] = _rms(hfin, fw_ref[...])


def _peer_experts(xn, u_bf16, vt_bf16, r2, bw, ci, ai, h1, final_w, tt, ec):
    n, d = xn.shape
    ne = u_bf16.shape[0]
    gate = pl.BlockSpec((PEER_HEADS, PEER_NKEYS, tt), lambda t, c: (0, 0, t))
    return pl.pallas_call(
        functools.partial(_experts_body, groups=ec // PEER_NKEYS),
        grid=(n // tt, ne // ec),
        in_specs=[pl.BlockSpec((tt, d), lambda t, c: (t, 0)),
                  pl.BlockSpec((ec, d), lambda t, c: (c, 0)),
                  pl.BlockSpec((d, ec), lambda t, c: (0, c)),
                  gate, gate, gate, gate,
                  pl.BlockSpec((tt, d), lambda t, c: (t, 0)),
                  pl.BlockSpec((1, d), lambda t, c: (0, 0))],
        out_specs=pl.BlockSpec((tt, d), lambda t, c: (t, 0)),
        out_shape=jax.ShapeDtypeStruct((n, d), F32),
        scratch_shapes=[pltpu.VMEM((d, tt), F32), pltpu.VMEM((ec, tt), BF16)],
        compiler_params=_params("parallel", "arbitrary"),
        name="peer_experts",
    )(xn, u_bf16, vt_bf16, r2, bw, ci, ai, h1, final_w.reshape(1, d))


def _peer_key_blocks(keys):
    z = jnp.zeros_like(keys[:, 0])
    top = jnp.concatenate([keys[:, 0], z], axis=-1)
    bot = jnp.concatenate([z, keys[:, 1]], axis=-1)
    return jnp.concatenate([top, bot], axis=1).astype(BF16)


def kernel(x, norm1_w, w_in, b_gate, diff_lq1, diff_lk1, diff_lq2, diff_lk2, diff_subln_w, chunk_rel_bias,
           w_branch_diff, w_branch_chunk, w_out, norm2_w, peer_wq, peer_keys, peer_u, peer_v, final_norm_w):
    batch, seq, d = x.shape
    depth = norm1_w.shape[0]
    n = batch * seq
    h = x.reshape(n, d)
    rel_idx = _rel_row_index()
    for l in range(depth):
        lam_init = 0.8 - 0.6 * math.exp(-0.3 * l)
        qd, kd, vd, qc, kc, vc, gates = _in_proj(h, norm1_w[l], w_in[l].astype(BF16), tm=min(512, n))
        lam_vecs = jnp.stack([diff_lq1[l], diff_lk1[l], diff_lq2[l], diff_lk2[l]])
        od = _diff_attn(qd, kd, vd, lam_vecs, diff_subln_w[l], batch, seq, lam_init, tq=min(256, seq))
        oc = _chunk_attn(qc, kc, vc, chunk_rel_bias[l][:, rel_idx], batch, seq)
        h1, xn2, qp = _merge(od, oc, gates, h, b_gate[l], w_branch_diff[l].astype(BF16),
                             w_branch_chunk[l].astype(BF16), w_out[l].astype(BF16), norm2_w[l],
                             peer_wq[l].astype(BF16), tm=min(512, n))
        tt = min(512, n)
        r2, bw, ci, ai = _peer_route(qp, _peer_key_blocks(peer_keys[l]), tt)
        last = l == depth - 1
        fw = final_norm_w if last else jnp.ones((d,), F32)
        h = _peer_experts(xn2, peer_u[l].astype(BF16), peer_v[l].T.astype(BF16), r2, bw, ci, ai, h1, fw,
                          tt=tt, ec=512)
        assert last, "the fused final RMSNorm assumes a single layer"
    return h.reshape(batch, seq, d)
```

```python
import functools
import math

import jax
import jax.numpy as jnp
import numpy as np
from jax import lax
from jax.experimental import pallas as pl
from jax.experimental.pallas import tpu as pltpu

F32 = jnp.float32
BF16 = jnp.bfloat16

EPS = 1e-6
CHUNK = 64
DIFF_HEADS = 8
DIFF_DH = 64
CH_HEADS = 8
CH_DH = 64
CH_LEFT = 8
REL_CLIP = 128
PEER_HEADS = 8
PEER_NKEYS = 128
PEER_TOPK = 16
NEG = -1e30

VMEM_LIMIT = 56 * 1024 * 1024

_NT = (((1,), (1,)), ((), ()))


def _params(*sem):
    return pltpu.CompilerParams(dimension_semantics=sem, vmem_limit_bytes=VMEM_LIMIT)


def _rms(x, w):
    return x * lax.rsqrt(jnp.mean(x * x, axis=-1, keepdims=True) + EPS) * w


_QKV_WIDTHS = (1024, 1024, 1024, 512, 512, 512)
_Q_SCALED = (True, False, False, True, False, False)
_VD = 2


def _inproj_body(x_ref, nw_ref, w_ref, wvt_ref, qd_ref, kd_ref, vdt_ref, qc_ref, kc_ref, vc_ref, g_ref):
    xn = _rms(x_ref[...], nw_ref[...]).astype(BF16)
    col = 0
    outs = (qd_ref, kd_ref, None, qc_ref, kc_ref, vc_ref)
    for ref, width, scaled in zip(outs, _QKV_WIDTHS, _Q_SCALED):
        if ref is not None:
            acc = jnp.dot(xn, w_ref[:, col:col + width], preferred_element_type=F32)
            if scaled:
                acc = acc * (DIFF_DH ** -0.5)
            ref[...] = acc.astype(BF16)
        col += width
    g_ref[...] = jnp.dot(xn, w_ref[:, col:], preferred_element_type=F32)
    vdt_ref[0] = lax.dot_general(wvt_ref[...], xn, _NT, preferred_element_type=F32).astype(BF16)


def _in_proj(x2, norm_w, w_in_bf16, tm):
    n, d = x2.shape
    cols = w_in_bf16.shape[1]
    vcol = sum(_QKV_WIDTHS[:_VD])
    wvt = w_in_bf16[:, vcol:vcol + _QKV_WIDTHS[_VD]].T
    row = lambda width: pl.BlockSpec((tm, width), lambda i: (i, 0))
    const = lambda shape: pl.BlockSpec(shape, lambda i: (0, 0))
    out_specs = [row(w) for w in _QKV_WIDTHS] + [row(2 * d)]
    out_shape = [jax.ShapeDtypeStruct((n, w), BF16) for w in _QKV_WIDTHS] + [jax.ShapeDtypeStruct((n, 2 * d), F32)]
    out_specs[_VD] = pl.BlockSpec((1, _QKV_WIDTHS[_VD], tm), lambda i: (i, 0, 0))
    out_shape[_VD] = jax.ShapeDtypeStruct((n // tm, _QKV_WIDTHS[_VD], tm), BF16)
    return pl.pallas_call(
        _inproj_body,
        grid=(n // tm,),
        in_specs=[row(d), const((1, d)), const((d, cols)), const(wvt.shape)],
        out_specs=out_specs,
        out_shape=out_shape,
        compiler_params=_params("parallel"),
        name="in_proj",
    )(x2, norm_w.reshape(1, d), w_in_bf16, wvt)


def _diff_attn_body(q_ref, k_ref, vt_ref, lam_ref, subw_ref, o_ref, off_sc, diag_sc, *, t, lam_init):
    h = pl.program_id(1)
    i = pl.program_id(2)
    slope = lax.bitcast_convert_type(jnp.full((1, 1), 126 - h, jnp.int32) << 23, F32)

    @pl.when(i == 0)
    def _():
        kpos = lax.broadcasted_iota(jnp.int32, (t, t), 0)
        qpos = lax.broadcasted_iota(jnp.int32, (t, t), 1)
        dist = (qpos - kpos).astype(F32)
        off_sc[...] = -slope * dist
        diag_sc[...] = jnp.where((kpos // CHUNK) <= (qpos // CHUNK), -slope * jnp.abs(dist), NEG)

    q = q_ref[...]
    lane = lax.broadcasted_iota(jnp.int32, q.shape, 1)
    zero = jnp.zeros_like(q)
    qz = jnp.concatenate([jnp.where(lane < DIFF_DH, q, zero), jnp.where(lane >= DIFF_DH, q, zero)], axis=0)

    def tile(j, bias_ref, shift, carry):
        m_old, l_old, acc = carry
        kb = k_ref[pl.ds(pl.multiple_of(j * t, t), t), :]
        s = lax.dot_general(kb, qz, _NT, preferred_element_type=F32)
        bias = bias_ref[...]
        s = jnp.concatenate([s[:, :t] + bias, s[:, t:] + bias], axis=1)
        m_new = jnp.maximum(m_old, jnp.max(s, axis=0, keepdims=True) + shift)
        alpha = jnp.exp(m_old - m_new)
        p = jnp.exp(s - (m_new - shift))
        l_new = alpha * l_old + jnp.sum(p, axis=0, keepdims=True)
        acc = alpha * acc + jnp.dot(vt_ref[j], p.astype(BF16), preferred_element_type=F32)
        return m_new, l_new, acc

    def body(j, carry):
        shift = -slope * ((i - j) * t).astype(F32)
        return tile(j, off_sc, shift, carry)

    init = (jnp.full((1, 2 * t), NEG, F32), jnp.zeros((1, 2 * t), F32), jnp.zeros((2 * DIFF_DH, 2 * t), F32))
    carry = lax.fori_loop(0, i, body, init)
    _, l, acc = tile(i, diag_sc, jnp.zeros((1, 1), F32), carry)

    o = acc / l
    lv = lam_ref[...]
    lam = (jnp.exp(jnp.sum(lv[0:1] * lv[1:2], axis=-1, keepdims=True))
           - jnp.exp(jnp.sum(lv[2:3] * lv[3:4], axis=-1, keepdims=True)) + lam_init)
    o = o[:, :t] - lam * o[:, t:]
    y = o * lax.rsqrt(jnp.mean(o * o, axis=0, keepdims=True) + EPS) * (subw_ref[...] * (1.0 - lam_init))
    o_ref[...] = y.T.astype(BF16)


def _diff_attn(qd, kd, vdt, lam_vecs, subln_w, batch, seq, lam_init):
    n = qd.shape[0]
    t = vdt.shape[2]
    nq = seq // t
    hw = 2 * DIFF_DH
    return pl.pallas_call(
        functools.partial(_diff_attn_body, t=t, lam_init=lam_init),
        grid=(batch, DIFF_HEADS, nq),
        in_specs=[
            pl.BlockSpec((t, hw), lambda b, h, i: (b * nq + i, h)),
            pl.BlockSpec((seq, hw), lambda b, h, i: (b, h)),
            pl.BlockSpec((nq, hw, t), lambda b, h, i: (b, h, 0)),
            pl.BlockSpec((4, DIFF_DH), lambda b, h, i: (0, 0)),
            pl.BlockSpec((hw, 1), lambda b, h, i: (0, 0)),
        ],
        out_specs=pl.BlockSpec((t, hw), lambda b, h, i: (b * nq + i, h)),
        out_shape=jax.ShapeDtypeStruct((n, DIFF_HEADS * hw), BF16),
        scratch_shapes=[pltpu.VMEM((t, t), F32), pltpu.VMEM((t, t), F32)],
        compiler_params=_params("parallel", "parallel", "arbitrary"),
        name="diff_attn",
    )(qd, kd, vdt, lam_vecs, subln_w.reshape(hw, 1))


_CT = 256
_CWIN = 3 * _CT
_FW = 1024


def _rel_row_index():
    lanes = np.arange(_FW)
    u = np.where(lanes < _CWIN, lanes, lanes - _FW)
    rel = CH_LEFT * CHUNK - u
    return np.clip(rel, -REL_CLIP, REL_CLIP) + REL_CLIP


def _chunk_attn_body(q_ref, k0_ref, k1_ref, k2_ref, v0_ref, v1_ref, v2_ref, f_ref, o_ref, bias_sc):
    i = pl.program_id(1)

    @pl.when((pl.program_id(0) == 0) & (i == 0))
    def _():
        r = lax.broadcasted_iota(jnp.int32, (_CT, _CWIN), 0) // CHUNK
        c = lax.broadcasted_iota(jnp.int32, (_CT, _CWIN), 1) // CHUNK
        allowed = (c >= r) & (c <= r + CH_LEFT)
        for h in range(CH_HEADS):
            row = jnp.broadcast_to(f_ref[h:h + 1, :], (_CT, _FW))
            toeplitz = pltpu.roll(row, 0, 1, stride=1, stride_axis=0)
            bias_sc[h] = jnp.where(allowed, toeplitz[:, :_CWIN], NEG)

    q = q_ref[...]
    ks = (k0_ref[...], k1_ref[...], k2_ref[...])
    vs = (v0_ref[...], v1_ref[...], v2_ref[...])
    pen = [jnp.where(i - 2 + d >= 0, 0.0, NEG).astype(F32) for d in range(2)] + [None]
    outs = []
    for h in range(CH_HEADS):
        sl = slice(h * CH_DH, (h + 1) * CH_DH)
        qh = q[:, sl]
        parts = []
        for d in range(3):
            sd = lax.dot_general(qh, ks[d][:, sl], _NT, preferred_element_type=F32)
            parts.append(sd if pen[d] is None else sd + pen[d])
        s = jnp.concatenate(parts, axis=1) + bias_sc[h]
        m = jnp.max(s, axis=-1, keepdims=True)
        p = jnp.exp(s - m)
        l = jnp.sum(p, axis=-1, keepdims=True)
        pb = p.astype(BF16)
        o = jnp.dot(pb[:, :_CT], vs[0][:, sl], preferred_element_type=F32)
        o += jnp.dot(pb[:, _CT:2 * _CT], vs[1][:, sl], preferred_element_type=F32)
        o += jnp.dot(pb[:, 2 * _CT:], vs[2][:, sl], preferred_element_type=F32)
        outs.append(o / l)
    o_ref[...] = jnp.concatenate(outs, axis=1).astype(BF16)


def _chunk_attn(qc, kc, vc, rel_rows, batch, seq):
    n, w = qc.shape
    nq = seq // _CT
    win = lambda d: pl.BlockSpec((_CT, w), lambda b, i: (b * nq + jnp.maximum(i - 2 + d, 0), 0))
    return pl.pallas_call(
        _chunk_attn_body,
        grid=(batch, nq),
        in_specs=[pl.BlockSpec((_CT, w), lambda b, i: (b * nq + i, 0)),
                  win(0), win(1), win(2), win(0), win(1), win(2),
                  pl.BlockSpec((CH_HEADS, _FW), lambda b, i: (0, 0))],
        out_specs=pl.BlockSpec((_CT, w), lambda b, i: (b * nq + i, 0)),
        out_shape=jax.ShapeDtypeStruct((n, w), BF16),
        scratch_shapes=[pltpu.VMEM((CH_HEADS, _CT, _CWIN), F32)],
        compiler_params=_params("arbitrary", "arbitrary"),
        name="chunk_attn",
    )(qc, kc, kc, kc, vc, vc, vc, rel_rows)


def _merge_body(od_ref, oc_ref, g_ref, x_ref, bg_ref, wbd_ref, wbc_ref, wo_ref, n2_ref, wq_ref,
                h_ref, xn_ref, qp_ref):
    d = x_ref.shape[1]
    g = g_ref[...] + bg_ref[...]
    pd = jnp.dot(od_ref[...], wbd_ref[...], preferred_element_type=F32)
    pc = jnp.dot(oc_ref[...], wbc_ref[...], preferred_element_type=F32)
    merged = jax.nn.sigmoid(g[:, :d]) * pd + jax.nn.sigmoid(g[:, d:]) * pc
    h1 = x_ref[...] + jnp.dot(merged.astype(BF16), wo_ref[...], preferred_element_type=F32)
    h_ref[...] = h1
    xn = _rms(h1, n2_ref[...]).astype(BF16)
    xn_ref[...] = xn
    qp_ref[...] = jnp.dot(xn, wq_ref[...], preferred_element_type=F32).astype(BF16)


def _merge(od, oc, gates, x2, b_gate, wbd, wbc, wo, norm2_w, wq, tm):
    n, d = x2.shape
    row = lambda width: pl.BlockSpec((tm, width), lambda i: (i, 0))
    const = lambda shape: pl.BlockSpec(shape, lambda i: (0, 0))
    return pl.pallas_call(
        _merge_body,
        grid=(n // tm,),
        in_specs=[row(od.shape[1]), row(oc.shape[1]), row(2 * d), row(d), const((1, 2 * d)),
                  const(wbd.shape), const(wbc.shape), const(wo.shape), const((1, d)), const(wq.shape)],
        out_specs=[row(d), row(d), row(wq.shape[1])],
        out_shape=[jax.ShapeDtypeStruct((n, d), F32), jax.ShapeDtypeStruct((n, d), BF16),
                   jax.ShapeDtypeStruct((n, wq.shape[1]), BF16)],
        compiler_params=_params("parallel"),
        name="merge",
    )(od, oc, gates, x2, b_gate.reshape(1, 2 * d), wbd, wbc, wo, norm2_w.reshape(1, d), wq)


def _pair_list():
    return [(a, b) for a in range(PEER_TOPK) for b in range(PEER_TOPK) if (a + 1) * (b + 1) <= PEER_TOPK]


def _extract_top(s, steps):
    rows = s.shape[0]
    idx = lax.broadcasted_iota(jnp.int32, s.shape, 0).astype(F32)
    rank = jnp.full(s.shape, float(steps), F32)
    vals = []
    for step in range(steps):
        m = jnp.max(s, axis=0, keepdims=True)
        first = jnp.min(jnp.where(s == m, idx, float(rows)), axis=0, keepdims=True)
        sel = idx == first
        rank = jnp.where(sel, float(step), rank)
        s = jnp.where(sel, -jnp.inf, s)
        vals.append(m)
    return vals, rank


def _route_body(qp_ref, kz_ref, r2_ref, bw_ref, ci_ref, ai_ref):
    nk = PEER_NKEYS
    pairs = _pair_list()
    for h in range(PEER_HEADS):
        qh = qp_ref[:, h * 2 * CH_DH:(h + 1) * 2 * CH_DH]
        st = lax.dot_general(kz_ref[h], qh, _NT, preferred_element_type=F32)
        s1, s2 = st[:nk], st[nk:]
        v1, rank1 = _extract_top(s1, PEER_TOPK)
        v2, rank2 = _extract_top(s2, PEER_TOPK)
        cand = jnp.concatenate([v1[a] + v2[b] for a, b in pairs], axis=0)
        top = cand[0:1]
        _, crank = _extract_top(cand, PEER_TOPK)
        chosen = crank < float(PEER_TOPK)
        z = jnp.sum(jnp.where(chosen, jnp.exp(cand - top), 0.0), axis=0, keepdims=True)
        ci = jnp.zeros(rank1.shape, F32)
        row = 0
        for a in range(PEER_TOPK):
            nb = PEER_TOPK // (a + 1)
            cnt = jnp.sum(jnp.where(chosen[row:row + nb], 1.0, 0.0), axis=0, keepdims=True)
            ci = ci + jnp.where(rank1 == float(a), cnt, 0.0)
            row += nb
        r2_ref[h] = rank2.astype(BF16)
        bw_ref[h] = (jnp.exp(s2 - v2[0]) / z).astype(BF16)
        ci_ref[h] = ci
        ai_ref[h] = jnp.exp(s1 - v1[0])


def _peer_route(qp, kz, tt):
    n, w = qp.shape
    spec = pl.BlockSpec((PEER_HEADS, PEER_NKEYS, tt), lambda t: (0, 0, t))
    shp = lambda dt: jax.ShapeDtypeStruct((PEER_HEADS, PEER_NKEYS, n), dt)
    return pl.pallas_call(
        _route_body,
        grid=(n // tt,),
        in_specs=[pl.BlockSpec((tt, w), lambda t: (t, 0)),
                  pl.BlockSpec(kz.shape, lambda t: (0, 0, 0))],
        out_specs=[spec, spec, spec, spec],
        out_shape=[shp(BF16), shp(BF16), shp(F32), shp(F32)],
        compiler_params=_params("parallel"),
        name="peer_route",
    )(qp, kz)


def _experts_body(xn_ref, u_ref, vt_ref, r2_ref, bw_ref, ci_ref, ai_ref, h_ref, fw_ref, y_ref, acc_sc, act_sc,
                  *, groups):
    c = pl.program_id(1)
    nk = PEER_NKEYS

    @pl.when(c == 0)
    def _():
        acc_sc[...] = jnp.zeros(acc_sc.shape, F32)

    hid = lax.dot_general(u_ref[...], xn_ref[...], _NT, preferred_element_type=F32)
    for gi in range(groups):
        i = c * groups + gi
        w = None
        for h in range(PEER_HEADS):
            ci = ci_ref[h, pl.ds(i, 1), :].astype(BF16)
            ai = ai_ref[h, pl.ds(i, 1), :].astype(BF16)
            term = jnp.where(r2_ref[h] < ci, bw_ref[h], jnp.zeros((), BF16)) * ai
            w = term if w is None else w + term
        hg = hid[gi * nk:(gi + 1) * nk]
        gelu = 0.5 * hg * (1.0 + lax.erf(hg * (2.0 ** -0.5)))
        act_sc[gi * nk:(gi + 1) * nk, :] = (gelu * w.astype(F32)).astype(BF16)
    acc_sc[...] += jnp.dot(vt_ref[...], act_sc[...], preferred_element_type=F32)

    @pl.when(c == pl.num_programs(1) - 1)
    def _():
        hfin = h_ref[...] + acc_sc[...].T
        y_ref[...] = _rms(hfin, fw_ref[...])


def _peer_experts(xn, u_bf16, vt_bf16, r2, bw, ci, ai, h1, final_w, tt, ec):
    n, d = xn.shape
    ne = u_bf16.shape[0]
    gate = pl.BlockSpec((PEER_HEADS, PEER_NKEYS, tt), lambda t, c: (0, 0, t))
    return pl.pallas_call(
        functools.partial(_experts_body, groups=ec // PEER_NKEYS),
        grid=(n // tt, ne // ec),
        in_specs=[pl.BlockSpec((tt, d), lambda t, c: (t, 0)),
                  pl.BlockSpec((ec, d), lambda t, c: (c, 0)),
                  pl.BlockSpec((d, ec), lambda t, c: (0, c)),
                  gate, gate, gate, gate,
                  pl.BlockSpec((tt, d), lambda t, c: (t, 0)),
                  pl.BlockSpec((1, d), lambda t, c: (0, 0))],
        out_specs=pl.BlockSpec((tt, d), lambda t, c: (t, 0)),
        out_shape=jax.ShapeDtypeStruct((n, d), F32),
        scratch_shapes=[pltpu.VMEM((d, tt), F32), pltpu.VMEM((ec, tt), BF16)],
        compiler_params=_params("parallel", "arbitrary"),
        name="peer_experts",
    )(xn, u_bf16, vt_bf16, r2, bw, ci, ai, h1, final_w.reshape(1, d))


def _peer_key_blocks(keys):
    z = jnp.zeros_like(keys[:, 0])
    top = jnp.concatenate([keys[:, 0], z], axis=-1)
    bot = jnp.concatenate([z, keys[:, 1]], axis=-1)
    return jnp.concatenate([top, bot], axis=1).astype(BF16)


def kernel(x, norm1_w, w_in, b_gate, diff_lq1, diff_lk1, diff_lq2, diff_lk2, diff_subln_w, chunk_rel_bias,
           w_branch_diff, w_branch_chunk, w_out, norm2_w, peer_wq, peer_keys, peer_u, peer_v, final_norm_w):
    batch, seq, d = x.shape
    depth = norm1_w.shape[0]
    n = batch * seq
    h = x.reshape(n, d)
    rel_idx = _rel_row_index()
    for l in range(depth):
        lam_init = 0.8 - 0.6 * math.exp(-0.3 * l)
        qd, kd, vdt, qc, kc, vc, gates = _in_proj(h, norm1_w[l], w_in[l].astype(BF16), tm=min(512, seq))
        lam_vecs = jnp.stack([diff_lq1[l], diff_lk1[l], diff_lq2[l], diff_lk2[l]])
        od = _diff_attn(qd, kd, vdt, lam_vecs, diff_subln_w[l], batch, seq, lam_init)
        oc = _chunk_attn(qc, kc, vc, chunk_rel_bias[l][:, rel_idx], batch, seq)
        h1, xn2, qp = _merge(od, oc, gates, h, b_gate[l], w_branch_diff[l].astype(BF16),
                             w_branch_chunk[l].astype(BF16), w_out[l].astype(BF16), norm2_w[l],
                             peer_wq[l].astype(BF16), tm=min(512, n))
        tt = min(512, n)
        r2, bw, ci, ai = _peer_route(qp, _peer_key_blocks(peer_keys[l]), tt)
        last = l == depth - 1
        fw = final_norm_w if last else jnp.ones((d,), F32)
        h = _peer_experts(xn2, peer_u[l].astype(BF16), peer_v[l].T.astype(BF16), r2, bw, ci, ai, h1, fw,
                          tt=tt, ec=512)
        assert last, "the fused final RMSNorm assumes a single layer"
    return h.reshape(batch, seq, d)
```

```python
import functools
import math

import jax
import jax.numpy as jnp
import numpy as np
from jax import lax
from jax.experimental import pallas as pl
from jax.experimental.pallas import tpu as pltpu

F32 = jnp.float32
BF16 = jnp.bfloat16

EPS = 1e-6
CHUNK = 64
DIFF_HEADS = 8
DIFF_DH = 64
CH_HEADS = 8
CH_DH = 64
CH_LEFT = 8
REL_CLIP = 128
PEER_HEADS = 8
PEER_NKEYS = 128
PEER_TOPK = 16
NEG = -1e30

VMEM_LIMIT = 56 * 1024 * 1024

_NT = (((1,), (1,)), ((), ()))


def _params(*sem):
    return pltpu.CompilerParams(dimension_semantics=sem, vmem_limit_bytes=VMEM_LIMIT)


def _rms(x, w):
    return x * lax.rsqrt(jnp.mean(x * x, axis=-1, keepdims=True) + EPS) * w


_QKV_WIDTHS = (1024, 1024, 1024, 512, 512, 512)
LOG2E = math.log2(math.e)
_Q_SCALE = (DIFF_DH ** -0.5 * LOG2E, None, None, CH_DH ** -0.5, None, None)
_VD = 2


def _inproj_body(x_ref, nw_ref, w_ref, wvt_ref, qd_ref, kd_ref, vdt_ref, qc_ref, kc_ref, vc_ref, g_ref):
    xn = _rms(x_ref[...], nw_ref[...]).astype(BF16)
    col = 0
    outs = (qd_ref, kd_ref, None, qc_ref, kc_ref, vc_ref)
    for ref, width, scale in zip(outs, _QKV_WIDTHS, _Q_SCALE):
        if ref is not None:
            acc = jnp.dot(xn, w_ref[:, col:col + width], preferred_element_type=F32)
            if scale is not None:
                acc = acc * scale
            ref[...] = acc.astype(BF16)
        col += width
    g_ref[...] = jnp.dot(xn, w_ref[:, col:], preferred_element_type=F32)
    vdt_ref[0] = lax.dot_general(wvt_ref[...], xn, _NT, preferred_element_type=F32).astype(BF16)


def _in_proj(x2, norm_w, w_in_bf16, tm):
    n, d = x2.shape
    cols = w_in_bf16.shape[1]
    vcol = sum(_QKV_WIDTHS[:_VD])
    wvt = w_in_bf16[:, vcol:vcol + _QKV_WIDTHS[_VD]].T
    row = lambda width: pl.BlockSpec((tm, width), lambda i: (i, 0))
    const = lambda shape: pl.BlockSpec(shape, lambda i: (0, 0))
    out_specs = [row(w) for w in _QKV_WIDTHS] + [row(2 * d)]
    out_shape = [jax.ShapeDtypeStruct((n, w), BF16) for w in _QKV_WIDTHS] + [jax.ShapeDtypeStruct((n, 2 * d), F32)]
    out_specs[_VD] = pl.BlockSpec((1, _QKV_WIDTHS[_VD], tm), lambda i: (i, 0, 0))
    out_shape[_VD] = jax.ShapeDtypeStruct((n // tm, _QKV_WIDTHS[_VD], tm), BF16)
    return pl.pallas_call(
        _inproj_body,
        grid=(n // tm,),
        in_specs=[row(d), const((1, d)), const((d, cols)), const(wvt.shape)],
        out_specs=out_specs,
        out_shape=out_shape,
        compiler_params=_params("parallel"),
        name="in_proj",
    )(x2, norm_w.reshape(1, d), w_in_bf16, wvt)


_QS = 256


def _diff_attn_body(q_ref, k_ref, vt_ref, lam_ref, subw_ref, o_ref, off_sc, diag_sc, sa_sc, sb_sc, *, t,
                    lam_init):
    h = pl.program_id(1)
    i = pl.program_id(2)
    slope = lax.bitcast_convert_type(jnp.full((1, 1), 126 - h, jnp.int32) << 23, F32) * LOG2E

    @pl.when(i == 0)
    def _():
        kpos = lax.broadcasted_iota(jnp.int32, (t, t), 0)
        qpos = lax.broadcasted_iota(jnp.int32, (t, t), 1)
        dist = (qpos - kpos).astype(F32)
        off_sc[...] = -slope * dist
        diag_sc[...] = jnp.where((kpos // CHUNK) <= (qpos // CHUNK), -slope * jnp.abs(dist), NEG)

    q = q_ref[...]
    lane = lax.broadcasted_iota(jnp.int32, q.shape, 1)
    zero = jnp.zeros_like(q)
    qz = jnp.concatenate([jnp.where(lane < DIFF_DH, q, zero), jnp.where(lane >= DIFF_DH, q, zero)], axis=0)

    def scores(j, s_ref):
        kb = k_ref[pl.ds(pl.multiple_of(j * t, t), t), :]
        s_ref[...] = lax.dot_general(kb, qz, _NT, preferred_element_type=F32)

    def tile(j, s_ref, bias_ref, shift, carry):
        vtb = vt_ref[j]
        new = []
        for n in range(2 * t // _QS):
            lo, blo = n * _QS, (n * _QS) % t
            m_old, l_old, acc = (c[:, lo:lo + _QS] for c in carry)
            s = s_ref[:, lo:lo + _QS] + bias_ref[:, blo:blo + _QS]
            m_new = jnp.maximum(m_old, jnp.max(s, axis=0, keepdims=True) + shift)
            alpha = jnp.exp2(m_old - m_new)
            p = jnp.exp2(s - (m_new - shift))
            l_new = alpha * l_old + jnp.sum(p, axis=0, keepdims=True)
            acc = alpha * acc + jnp.dot(vtb, p.astype(BF16), preferred_element_type=F32)
            new.append((m_new, l_new, acc))
        return tuple(jnp.concatenate(parts, axis=1) for parts in zip(*new))

    def off_tile(j, s_ref, carry):
        return tile(j, s_ref, off_sc, -slope * ((i - j) * t).astype(F32), carry)

    def diag_tile(s_ref, carry):
        return tile(i, s_ref, diag_sc, jnp.zeros((1, 1), F32), carry)

    def pair(jj, carry):
        j = 2 * jj
        scores(j + 1, sb_sc)
        carry = off_tile(j, sa_sc, carry)
        scores(j + 2, sa_sc)
        return off_tile(j + 1, sb_sc, carry)

    def odd_tail(carry):
        scores(i, sb_sc)
        return diag_tile(sb_sc, off_tile(i - 1, sa_sc, carry))

    init = (jnp.full((1, 2 * t), NEG, F32), jnp.zeros((1, 2 * t), F32), jnp.zeros((2 * DIFF_DH, 2 * t), F32))
    scores(0, sa_sc)
    carry = lax.fori_loop(0, i // 2, pair, init)
    _, l, acc = lax.cond(i % 2 == 1, odd_tail, functools.partial(diag_tile, sa_sc), carry)

    o = acc / l
    lv = lam_ref[...]
    lam = (jnp.exp(jnp.sum(lv[0:1] * lv[1:2], axis=-1, keepdims=True))
           - jnp.exp(jnp.sum(lv[2:3] * lv[3:4], axis=-1, keepdims=True)) + lam_init)
    o = o[:, :t] - lam * o[:, t:]
    y = o * lax.rsqrt(jnp.mean(o * o, axis=0, keepdims=True) + EPS) * (subw_ref[...] * (1.0 - lam_init))
    o_ref[...] = y.T.astype(BF16)


def _diff_attn(qd, kd, vdt, lam_vecs, subln_w, batch, seq, lam_init):
    n = qd.shape[0]
    t = vdt.shape[2]
    nq = seq // t
    hw = 2 * DIFF_DH
    return pl.pallas_call(
        functools.partial(_diff_attn_body, t=t, lam_init=lam_init),
        grid=(batch, DIFF_HEADS, nq),
        in_specs=[
            pl.BlockSpec((t, hw), lambda b, h, i: (b * nq + i, h)),
            pl.BlockSpec((seq, hw), lambda b, h, i: (b, h)),
            pl.BlockSpec((nq, hw, t), lambda b, h, i: (b, h, 0)),
            pl.BlockSpec((4, DIFF_DH), lambda b, h, i: (0, 0)),
            pl.BlockSpec((hw, 1), lambda b, h, i: (0, 0)),
        ],
        out_specs=pl.BlockSpec((t, hw), lambda b, h, i: (b * nq + i, h)),
        out_shape=jax.ShapeDtypeStruct((n, DIFF_HEADS * hw), BF16),
        scratch_shapes=[pltpu.VMEM((t, t), F32), pltpu.VMEM((t, t), F32),
                        pltpu.VMEM((t, 2 * t), F32), pltpu.VMEM((t, 2 * t), F32)],
        compiler_params=_params("parallel", "parallel", "arbitrary"),
        name="diff_attn",
    )(qd, kd, vdt, lam_vecs, subln_w.reshape(hw, 1))


_CT = 256
_CWIN = 3 * _CT
_FW = 1024


def _rel_row_index():
    lanes = np.arange(_FW)
    u = np.where(lanes < _CWIN, lanes, lanes - _FW)
    rel = CH_LEFT * CHUNK - u
    return np.clip(rel, -REL_CLIP, REL_CLIP) + REL_CLIP


def _chunk_attn_body(q_ref, k0_ref, k1_ref, k2_ref, v0_ref, v1_ref, v2_ref, f_ref, o_ref, bias_sc):
    i = pl.program_id(1)

    @pl.when((pl.program_id(0) == 0) & (i == 0))
    def _():
        r = lax.broadcasted_iota(jnp.int32, (_CT, _CWIN), 0) // CHUNK
        c = lax.broadcasted_iota(jnp.int32, (_CT, _CWIN), 1) // CHUNK
        allowed = (c >= r) & (c <= r + CH_LEFT)
        for h in range(CH_HEADS):
            row = jnp.broadcast_to(f_ref[h:h + 1, :], (_CT, _FW))
            toeplitz = pltpu.roll(row, 0, 1, stride=1, stride_axis=0)
            bias_sc[h] = jnp.where(allowed, toeplitz[:, :_CWIN], NEG)

    q = q_ref[...]
    ks = (k0_ref[...], k1_ref[...], k2_ref[...])
    vs = (v0_ref[...], v1_ref[...], v2_ref[...])
    pen = [jnp.where(i - 2 + d >= 0, 0.0, NEG).astype(F32) for d in range(2)] + [None]
    outs = []
    for h in range(CH_HEADS):
        sl = slice(h * CH_DH, (h + 1) * CH_DH)
        qh = q[:, sl]
        parts = []
        for d in range(3):
            sd = lax.dot_general(qh, ks[d][:, sl], _NT, preferred_element_type=F32)
            parts.append(sd if pen[d] is None else sd + pen[d])
        s = jnp.concatenate(parts, axis=1) + bias_sc[h]
        m = jnp.max(s, axis=-1, keepdims=True)
        p = jnp.exp(s - m)
        l = jnp.sum(p, axis=-1, keepdims=True)
        pb = p.astype(BF16)
        o = jnp.dot(pb[:, :_CT], vs[0][:, sl], preferred_element_type=F32)
        o += jnp.dot(pb[:, _CT:2 * _CT], vs[1][:, sl], preferred_element_type=F32)
        o += jnp.dot(pb[:, 2 * _CT:], vs[2][:, sl], preferred_element_type=F32)
        outs.append(o / l)
    o_ref[...] = jnp.concatenate(outs, axis=1).astype(BF16)


def _chunk_attn(qc, kc, vc, rel_rows, batch, seq):
    n, w = qc.shape
    nq = seq // _CT
    win = lambda d: pl.BlockSpec((_CT, w), lambda b, i: (b * nq + jnp.maximum(i - 2 + d, 0), 0))
    return pl.pallas_call(
        _chunk_attn_body,
        grid=(batch, nq),
        in_specs=[pl.BlockSpec((_CT, w), lambda b, i: (b * nq + i, 0)),
                  win(0), win(1), win(2), win(0), win(1), win(2),
                  pl.BlockSpec((CH_HEADS, _FW), lambda b, i: (0, 0))],
        out_specs=pl.BlockSpec((_CT, w), lambda b, i: (b * nq + i, 0)),
        out_shape=jax.ShapeDtypeStruct((n, w), BF16),
        scratch_shapes=[pltpu.VMEM((CH_HEADS, _CT, _CWIN), F32)],
        compiler_params=_params("arbitrary", "arbitrary"),
        name="chunk_attn",
    )(qc, kc, kc, kc, vc, vc, vc, rel_rows)


def _merge_body(od_ref, oc_ref, g_ref, x_ref, bg_ref, wbd_ref, wbc_ref, wo_ref, n2_ref, wq_ref,
                h_ref, xn_ref, qp_ref):
    d = x_ref.shape[1]
    g = g_ref[...] + bg_ref[...]
    pd = jnp.dot(od_ref[...], wbd_ref[...], preferred_element_type=F32)
    pc = jnp.dot(oc_ref[...], wbc_ref[...], preferred_element_type=F32)
    merged = jax.nn.sigmoid(g[:, :d]) * pd + jax.nn.sigmoid(g[:, d:]) * pc
    h1 = x_ref[...] + jnp.dot(merged.astype(BF16), wo_ref[...], preferred_element_type=F32)
    h_ref[...] = h1
    xn = _rms(h1, n2_ref[...]).astype(BF16)
    xn_ref[...] = xn
    qp_ref[...] = jnp.dot(xn, wq_ref[...], preferred_element_type=F32).astype(BF16)


def _merge(od, oc, gates, x2, b_gate, wbd, wbc, wo, norm2_w, wq, tm):
    n, d = x2.shape
    row = lambda width: pl.BlockSpec((tm, width), lambda i: (i, 0))
    const = lambda shape: pl.BlockSpec(shape, lambda i: (0, 0))
    return pl.pallas_call(
        _merge_body,
        grid=(n // tm,),
        in_specs=[row(od.shape[1]), row(oc.shape[1]), row(2 * d), row(d), const((1, 2 * d)),
                  const(wbd.shape), const(wbc.shape), const(wo.shape), const((1, d)), const(wq.shape)],
        out_specs=[row(d), row(d), row(wq.shape[1])],
        out_shape=[jax.ShapeDtypeStruct((n, d), F32), jax.ShapeDtypeStruct((n, d), BF16),
                   jax.ShapeDtypeStruct((n, wq.shape[1]), BF16)],
        compiler_params=_params("parallel"),
        name="merge",
    )(od, oc, gates, x2, b_gate.reshape(1, 2 * d), wbd, wbc, wo, norm2_w.reshape(1, d), wq)


def _pair_list():
    return [(a, b) for a in range(PEER_TOPK) for b in range(PEER_TOPK) if (a + 1) * (b + 1) <= PEER_TOPK]


def _extract_top(s, steps):
    rows = s.shape[0]
    idx = lax.broadcasted_iota(jnp.int32, s.shape, 0).astype(F32)
    rank = jnp.full(s.shape, float(steps), F32)
    vals = []
    for step in range(steps):
        m = jnp.max(s, axis=0, keepdims=True)
        first = jnp.min(jnp.where(s == m, idx, float(rows)), axis=0, keepdims=True)
        sel = idx == first
        rank = jnp.where(sel, float(step), rank)
        s = jnp.where(sel, -jnp.inf, s)
        vals.append(m)
    return vals, rank


def _route_body(qp_ref, kz_ref, r2_ref, bw_ref, ci_ref, ai_ref):
    nk = PEER_NKEYS
    pairs = _pair_list()
    st = lax.dot_general(kz_ref[0], qp_ref[...], _NT, preferred_element_type=F32)
    s1, s2 = st[:nk], st[nk:]
    v1, rank1 = _extract_top(s1, PEER_TOPK)
    v2, rank2 = _extract_top(s2, PEER_TOPK)
    cand = jnp.concatenate([v1[a] + v2[b] for a, b in pairs], axis=0)
    top = cand[0:1]
    _, crank = _extract_top(cand, PEER_TOPK)
    chosen = crank < float(PEER_TOPK)
    z = jnp.sum(jnp.where(chosen, jnp.exp(cand - top), 0.0), axis=0, keepdims=True)
    ci = jnp.zeros(rank1.shape, F32)
    row = 0
    for a in range(PEER_TOPK):
        nb = PEER_TOPK // (a + 1)
        cnt = jnp.sum(jnp.where(chosen[row:row + nb], 1.0, 0.0), axis=0, keepdims=True)
        ci = ci + jnp.where(rank1 == float(a), cnt, 0.0)
        row += nb
    r2_ref[0] = rank2.astype(BF16)
    bw_ref[0] = (jnp.exp(s2 - v2[0]) / z).astype(BF16)
    ci_ref[0] = ci
    ai_ref[0] = jnp.exp(s1 - v1[0])


def _peer_route(qp, kz, tt):
    n, w = qp.shape
    hw = w // PEER_HEADS
    spec = pl.BlockSpec((1, PEER_NKEYS, tt), lambda t, h: (h, 0, t))
    shp = lambda dt: jax.ShapeDtypeStruct((PEER_HEADS, PEER_NKEYS, n), dt)
    return pl.pallas_call(
        _route_body,
        grid=(n // tt, PEER_HEADS),
        in_specs=[pl.BlockSpec((tt, hw), lambda t, h: (t, h)),
                  pl.BlockSpec((1,) + kz.shape[1:], lambda t, h: (h, 0, 0))],
        out_specs=[spec, spec, spec, spec],
        out_shape=[shp(BF16), shp(BF16), shp(F32), shp(F32)],
        compiler_params=_params("parallel", "parallel"),
        name="peer_route",
    )(qp, kz)


_EG = 256
_TB = 256


def _experts_body(xn_ref, u_ref, vt_ref, r2_ref, bw_ref, ci_ref, ai_ref, h_ref, fw_ref, y_ref, acc_sc, *, ec):
    c = pl.program_id(1)
    nk = PEER_NKEYS

    @pl.when(c == 0)
    def _():
        acc_sc[...] = jnp.zeros(acc_sc.shape, F32)

    xn = xn_ref[...]
    ngroups = ec // _EG

    def hidden(g):
        return lax.dot_general(u_ref[g * _EG:(g + 1) * _EG, :], xn, _NT, preferred_element_type=F32)

    total = None
    nxt = hidden(0)
    for g in range(ngroups):
        hid = nxt
        if g + 1 < ngroups:
            nxt = hidden(g + 1)
        acts = []
        for gi in range(_EG // nk):
            i = (c * ec + g * _EG) // nk + gi
            cis = [ci_ref[h, pl.ds(i, 1), :].astype(BF16) for h in range(PEER_HEADS)]
            ais = [ai_ref[h, pl.ds(i, 1), :].astype(BF16) for h in range(PEER_HEADS)]
            cols = []
            for lo in range(0, xn.shape[0], _TB):
                w = None
                for h in range(PEER_HEADS):
                    term = jnp.where(r2_ref[h, :, lo:lo + _TB] < cis[h][:, lo:lo + _TB],
                                     bw_ref[h, :, lo:lo + _TB], jnp.zeros((), BF16)) * ais[h][:, lo:lo + _TB]
                    w = term if w is None else w + term
                hg = hid[gi * nk:(gi + 1) * nk, lo:lo + _TB]
                gelu = 0.5 * hg * (1.0 + lax.erf(hg * (2.0 ** -0.5)))
                cols.append((gelu * w.astype(F32)).astype(BF16))
            acts.append(jnp.concatenate(cols, axis=1))
        act = jnp.concatenate(acts, axis=0)
        part = jnp.dot(vt_ref[:, g * _EG:(g + 1) * _EG], act, preferred_element_type=F32)
        total = part if total is None else total + part
    acc_sc[...] += total

    @pl.when(c == pl.num_programs(1) - 1)
    def _():
        hfin = h_ref[...] + acc_sc[...].T
        y_ref[...] = _rms(hfin, fw_ref[...])


def _peer_experts(xn, u_bf16, vt_bf16, r2, bw, ci, ai, h1, final_w, tt, ec):
    n, d = xn.shape
    ne = u_bf16.shape[0]
    gate = pl.BlockSpec((PEER_HEADS, PEER_NKEYS, tt), lambda t, c: (0, 0, t))
    return pl.pallas_call(
        functools.partial(_experts_body, ec=ec),
        grid=(n // tt, ne // ec),
        in_specs=[pl.BlockSpec((tt, d), lambda t, c: (t, 0)),
                  pl.BlockSpec((ec, d), lambda t, c: (c, 0)),
                  pl.BlockSpec((d, ec), lambda t, c: (0, c)),
                  gate, gate, gate, gate,
                  pl.BlockSpec((tt, d), lambda t, c: (t, 0)),
                  pl.BlockSpec((1, d), lambda t, c: (0, 0))],
        out_specs=pl.BlockSpec((tt, d), lambda t, c: (t, 0)),
        out_shape=jax.ShapeDtypeStruct((n, d), F32),
        scratch_shapes=[pltpu.VMEM((d, tt), F32)],
        compiler_params=_params("parallel", "arbitrary"),
        name="peer_experts",
    )(xn, u_bf16, vt_bf16, r2, bw, ci, ai, h1, final_w.reshape(1, d))


def _peer_key_blocks(keys):
    z = jnp.zeros_like(keys[:, 0])
    top = jnp.concatenate([keys[:, 0], z], axis=-1)
    bot = jnp.concatenate([z, keys[:, 1]], axis=-1)
    return jnp.concatenate([top, bot], axis=1).astype(BF16)


def kernel(x, norm1_w, w_in, b_gate, diff_lq1, diff_lk1, diff_lq2, diff_lk2, diff_subln_w, chunk_rel_bias,
           w_branch_diff, w_branch_chunk, w_out, norm2_w, peer_wq, peer_keys, peer_u, peer_v, final_norm_w):
    batch, seq, d = x.shape
    depth = norm1_w.shape[0]
    n = batch * seq
    h = x.reshape(n, d)
    rel_idx = _rel_row_index()
    for l in range(depth):
        lam_init = 0.8 - 0.6 * math.exp(-0.3 * l)
        qd, kd, vdt, qc, kc, vc, gates = _in_proj(h, norm1_w[l], w_in[l].astype(BF16), tm=min(512, seq))
        lam_vecs = jnp.stack([diff_lq1[l], diff_lk1[l], diff_lq2[l], diff_lk2[l]])
        od = _diff_attn(qd, kd, vdt, lam_vecs, diff_subln_w[l], batch, seq, lam_init)
        oc = _chunk_attn(qc, kc, vc, chunk_rel_bias[l][:, rel_idx], batch, seq)
        h1, xn2, qp = _merge(od, oc, gates, h, b_gate[l], w_branch_diff[l].astype(BF16),
                             w_branch_chunk[l].astype(BF16), w_out[l].astype(BF16), norm2_w[l],
                             peer_wq[l].astype(BF16), tm=min(512, n))
        tt = min(512, n)
        r2, bw, ci, ai = _peer_route(qp, _peer_key_blocks(peer_keys[l]), tt)
        last = l == depth - 1
        fw = final_norm_w if last else jnp.ones((d,), F32)
        h = _peer_experts(xn2, peer_u[l].astype(BF16), peer_v[l].T.astype(BF16), r2, bw, ci, ai, h1, fw,
                          tt=tt, ec=1024)
        assert last, "the fused final RMSNorm assumes a single layer"
    return h.reshape(batch, seq, d)
```

```python
import functools
import math

import jax
import jax.numpy as jnp
import numpy as np
from jax import lax
from jax.experimental import pallas as pl
from jax.experimental.pallas import tpu as pltpu

F32 = jnp.float32
BF16 = jnp.bfloat16

EPS = 1e-6
CHUNK = 64
DIFF_HEADS = 8
DIFF_DH = 64
CH_HEADS = 8
CH_DH = 64
CH_LEFT = 8
REL_CLIP = 128
PEER_HEADS = 8
PEER_NKEYS = 128
PEER_TOPK = 16
NEG = -1e30
_BT = 16

VMEM_LIMIT = 56 * 1024 * 1024

_NT = (((1,), (1,)), ((), ()))


def _params(*sem):
    return pltpu.CompilerParams(dimension_semantics=sem, vmem_limit_bytes=VMEM_LIMIT)


def _rms(x, w):
    return x * lax.rsqrt(jnp.mean(x * x, axis=-1, keepdims=True) + EPS) * w


_QKV_WIDTHS = (1024, 1024, 1024, 512, 512, 512)
LOG2E = math.log2(math.e)
_Q_SCALE = (DIFF_DH ** -0.5 * LOG2E, None, None, CH_DH ** -0.5, None, None)
_VD = 2


def _inproj_body(x_ref, nw_ref, w_ref, wvt_ref, qd_ref, kd_ref, vdt_ref, qc_ref, kc_ref, vc_ref, g_ref):
    xn = _rms(x_ref[...], nw_ref[...]).astype(BF16)
    col = 0
    outs = (qd_ref, kd_ref, None, qc_ref, kc_ref, vc_ref)
    for ref, width, scale in zip(outs, _QKV_WIDTHS, _Q_SCALE):
        if ref is not None:
            acc = jnp.dot(xn, w_ref[:, col:col + width], preferred_element_type=F32)
            if scale is not None:
                acc = acc * scale
            ref[...] = acc.astype(BF16)
        col += width
    g_ref[...] = jnp.dot(xn, w_ref[:, col:], preferred_element_type=F32)
    vdt_ref[0] = lax.dot_general(wvt_ref[...], xn, _NT, preferred_element_type=F32).astype(BF16)


def _in_proj(x2, norm_w, w_in_bf16, tm):
    n, d = x2.shape
    cols = w_in_bf16.shape[1]
    vcol = sum(_QKV_WIDTHS[:_VD])
    wvt = w_in_bf16[:, vcol:vcol + _QKV_WIDTHS[_VD]].T
    row = lambda width: pl.BlockSpec((tm, width), lambda i: (i, 0))
    const = lambda shape: pl.BlockSpec(shape, lambda i: (0, 0))
    out_specs = [row(w) for w in _QKV_WIDTHS] + [row(2 * d)]
    out_shape = [jax.ShapeDtypeStruct((n, w), BF16) for w in _QKV_WIDTHS] + [jax.ShapeDtypeStruct((n, 2 * d), F32)]
    out_specs[_VD] = pl.BlockSpec((1, _QKV_WIDTHS[_VD], tm), lambda i: (i, 0, 0))
    out_shape[_VD] = jax.ShapeDtypeStruct((n // tm, _QKV_WIDTHS[_VD], tm), BF16)
    return pl.pallas_call(
        _inproj_body,
        grid=(n // tm,),
        in_specs=[row(d), const((1, d)), const((d, cols)), const(wvt.shape)],
        out_specs=out_specs,
        out_shape=out_shape,
        compiler_params=_params("parallel"),
        name="in_proj",
    )(x2, norm_w.reshape(1, d), w_in_bf16, wvt)


_QS = 256


def _diff_attn_body(q_ref, k_ref, vt_ref, lam_ref, subw_ref, o_ref, off_sc, diag_sc, sa_sc, sb_sc, *, t,
                    lam_init):
    h = pl.program_id(1)
    i = pl.program_id(2)
    slope = lax.bitcast_convert_type(jnp.full((1, 1), 126 - h, jnp.int32) << 23, F32) * LOG2E

    @pl.when(i == 0)
    def _():
        kpos = lax.broadcasted_iota(jnp.int32, (t, t), 0)
        qpos = lax.broadcasted_iota(jnp.int32, (t, t), 1)
        dist = (qpos - kpos).astype(F32)
        off_sc[...] = -slope * dist
        diag_sc[...] = jnp.where((kpos // CHUNK) <= (qpos // CHUNK), -slope * jnp.abs(dist), NEG)

    q = q_ref[...]
    lane = lax.broadcasted_iota(jnp.int32, q.shape, 1)
    zero = jnp.zeros_like(q)
    qz = jnp.concatenate([jnp.where(lane < DIFF_DH, q, zero), jnp.where(lane >= DIFF_DH, q, zero)], axis=0)

    def scores(j, bias_ref, s_ref):
        kb = k_ref[pl.ds(pl.multiple_of(j * t, t), t), :]
        s = lax.dot_general(kb, qz, _NT, preferred_element_type=F32)
        bias = bias_ref[...]
        s_ref[:, :t] = s[:, :t] + bias
        s_ref[:, t:] = s[:, t:] + bias

    def tile(j, s_ref, shift, carry):
        vtb = vt_ref[j]
        new = []
        for n in range(2 * t // _QS):
            lo = n * _QS
            m_old, l_old, acc = (c[:, lo:lo + _QS] for c in carry)
            m_new = jnp.maximum(m_old, jnp.max(s_ref[:, lo:lo + _QS], axis=0, keepdims=True) + shift)
            alpha = jnp.exp2(m_old - m_new)
            p = jnp.exp2(s_ref[:, lo:lo + _QS] - (m_new - shift))
            l_new = alpha * l_old + jnp.sum(p, axis=0, keepdims=True)
            acc = alpha * acc + jnp.dot(vtb, p.astype(BF16), preferred_element_type=F32)
            new.append((m_new, l_new, acc))
        return tuple(jnp.concatenate(parts, axis=1) for parts in zip(*new))

    def off_tile(o, s_ref, carry):
        return tile(o, s_ref, -slope * ((i - o) * t).astype(F32), carry)

    def pair(p, carry):
        o = 2 * p
        scores(o + 1, off_sc, sa_sc)
        carry = off_tile(o, sb_sc, carry)
        scores(jnp.minimum(o + 2, i - 1), off_sc, sb_sc)
        return off_tile(o + 1, sa_sc, carry)

    init = (jnp.full((1, 2 * t), NEG, F32), jnp.zeros((1, 2 * t), F32), jnp.zeros((2 * DIFF_DH, 2 * t), F32))
    scores(i, diag_sc, sa_sc)
    scores(0, off_sc, sb_sc)
    carry = tile(i, sa_sc, jnp.zeros((1, 1), F32), init)
    carry = lax.fori_loop(0, i // 2, pair, carry)
    _, l, acc = lax.cond(i % 2 == 1, functools.partial(off_tile, i - 1, sb_sc), lambda c: c, carry)

    o = acc / l
    lv = lam_ref[...]
    lam = (jnp.exp(jnp.sum(lv[0:1] * lv[1:2], axis=-1, keepdims=True))
           - jnp.exp(jnp.sum(lv[2:3] * lv[3:4], axis=-1, keepdims=True)) + lam_init)
    o = o[:, :t] - lam * o[:, t:]
    y = o * lax.rsqrt(jnp.mean(o * o, axis=0, keepdims=True) + EPS) * (subw_ref[...] * (1.0 - lam_init))
    o_ref[...] = y.T.astype(BF16)


def _diff_attn(qd, kd, vdt, lam_vecs, subln_w, batch, seq, lam_init):
    n = qd.shape[0]
    t = vdt.shape[2]
    nq = seq // t
    hw = 2 * DIFF_DH
    return pl.pallas_call(
        functools.partial(_diff_attn_body, t=t, lam_init=lam_init),
        grid=(batch, DIFF_HEADS, nq),
        in_specs=[
            pl.BlockSpec((t, hw), lambda b, h, i: (b * nq + i, h)),
            pl.BlockSpec((seq, hw), lambda b, h, i: (b, h)),
            pl.BlockSpec((nq, hw, t), lambda b, h, i: (b, h, 0)),
            pl.BlockSpec((4, DIFF_DH), lambda b, h, i: (0, 0)),
            pl.BlockSpec((hw, 1), lambda b, h, i: (0, 0)),
        ],
        out_specs=pl.BlockSpec((t, hw), lambda b, h, i: (b * nq + i, h)),
        out_shape=jax.ShapeDtypeStruct((n, DIFF_HEADS * hw), BF16),
        scratch_shapes=[pltpu.VMEM((t, t), F32), pltpu.VMEM((t, t), F32),
                        pltpu.VMEM((t, 2 * t), F32), pltpu.VMEM((t, 2 * t), F32)],
        compiler_params=_params("parallel", "parallel", "arbitrary"),
        name="diff_attn",
    )(qd, kd, vdt, lam_vecs, subln_w.reshape(hw, 1))


_CT = 256
_CWIN = 3 * _CT
_FW = 1024


def _rel_row_index():
    lanes = np.arange(_FW)
    u = np.where(lanes < _CWIN, lanes, lanes - _FW)
    rel = CH_LEFT * CHUNK - u
    return np.clip(rel, -REL_CLIP, REL_CLIP) + REL_CLIP


def _chunk_attn_body(q_ref, k0_ref, k1_ref, k2_ref, v0_ref, v1_ref, v2_ref, f_ref, o_ref, bias_sc):
    i = pl.program_id(1)

    @pl.when((pl.program_id(0) == 0) & (i == 0))
    def _():
        r = lax.broadcasted_iota(jnp.int32, (_CT, _CWIN), 0) // CHUNK
        c = lax.broadcasted_iota(jnp.int32, (_CT, _CWIN), 1) // CHUNK
        allowed = (c >= r) & (c <= r + CH_LEFT)
        for h in range(CH_HEADS):
            row = jnp.broadcast_to(f_ref[h:h + 1, :], (_CT, _FW))
            toeplitz = pltpu.roll(row, 0, 1, stride=1, stride_axis=0)
            bias_sc[h] = jnp.where(allowed, toeplitz[:, :_CWIN], NEG)

    q = q_ref[...]
    ks = (k0_ref[...], k1_ref[...], k2_ref[...])
    vs = (v0_ref[...], v1_ref[...], v2_ref[...])
    pen = [jnp.where(i - 2 + d >= 0, 0.0, NEG).astype(F32) for d in range(2)] + [None]
    outs = []
    for h in range(CH_HEADS):
        sl = slice(h * CH_DH, (h + 1) * CH_DH)
        qh = q[:, sl]
        parts = []
        for d in range(3):
            sd = lax.dot_general(qh, ks[d][:, sl], _NT, preferred_element_type=F32)
            parts.append(sd if pen[d] is None else sd + pen[d])
        s = jnp.concatenate(parts, axis=1) + bias_sc[h]
        m = jnp.max(s, axis=-1, keepdims=True)
        p = jnp.exp(s - m)
        l = jnp.sum(p, axis=-1, keepdims=True)
        pb = p.astype(BF16)
        o = jnp.dot(pb[:, :_CT], vs[0][:, sl], preferred_element_type=F32)
        o += jnp.dot(pb[:, _CT:2 * _CT], vs[1][:, sl], preferred_element_type=F32)
        o += jnp.dot(pb[:, 2 * _CT:], vs[2][:, sl], preferred_element_type=F32)
        outs.append(o / l)
    o_ref[...] = jnp.concatenate(outs, axis=1).astype(BF16)


def _chunk_attn(qc, kc, vc, rel_rows, batch, seq):
    n, w = qc.shape
    nq = seq // _CT
    win = lambda d: pl.BlockSpec((_CT, w), lambda b, i: (b * nq + jnp.maximum(i - 2 + d, 0), 0))
    return pl.pallas_call(
        _chunk_attn_body,
        grid=(batch, nq),
        in_specs=[pl.BlockSpec((_CT, w), lambda b, i: (b * nq + i, 0)),
                  win(0), win(1), win(2), win(0), win(1), win(2),
                  pl.BlockSpec((CH_HEADS, _FW), lambda b, i: (0, 0))],
        out_specs=pl.BlockSpec((_CT, w), lambda b, i: (b * nq + i, 0)),
        out_shape=jax.ShapeDtypeStruct((n, w), BF16),
        scratch_shapes=[pltpu.VMEM((CH_HEADS, _CT, _CWIN), F32)],
        compiler_params=_params("arbitrary", "arbitrary"),
        name="chunk_attn",
    )(qc, kc, kc, kc, vc, vc, vc, rel_rows)


def _merge_body(od_ref, oc_ref, g_ref, x_ref, bg_ref, wbd_ref, wbc_ref, wo_ref, n2_ref, wq_ref,
                h_ref, xn_ref, qp_ref):
    d = x_ref.shape[1]
    g = g_ref[...] + bg_ref[...]
    pd = jnp.dot(od_ref[...], wbd_ref[...], preferred_element_type=F32)
    pc = jnp.dot(oc_ref[...], wbc_ref[...], preferred_element_type=F32)
    merged = jax.nn.sigmoid(g[:, :d]) * pd + jax.nn.sigmoid(g[:, d:]) * pc
    h1 = x_ref[...] + jnp.dot(merged.astype(BF16), wo_ref[...], preferred_element_type=F32)
    h_ref[...] = h1
    xn = _rms(h1, n2_ref[...]).astype(BF16)
    xn_ref[...] = xn
    qp_ref[...] = jnp.dot(xn, wq_ref[...], preferred_element_type=F32).astype(BF16)


def _merge(od, oc, gates, x2, b_gate, wbd, wbc, wo, norm2_w, wq, tm):
    n, d = x2.shape
    row = lambda width: pl.BlockSpec((tm, width), lambda i: (i, 0))
    const = lambda shape: pl.BlockSpec(shape, lambda i: (0, 0))
    return pl.pallas_call(
        _merge_body,
        grid=(n // tm,),
        in_specs=[row(od.shape[1]), row(oc.shape[1]), row(2 * d), row(d), const((1, 2 * d)),
                  const(wbd.shape), const(wbc.shape), const(wo.shape), const((1, d)), const(wq.shape)],
        out_specs=[row(d), row(d), row(wq.shape[1])],
        out_shape=[jax.ShapeDtypeStruct((n, d), F32), jax.ShapeDtypeStruct((n, d), BF16),
                   jax.ShapeDtypeStruct((n, wq.shape[1]), BF16)],
        compiler_params=_params("parallel"),
        name="merge",
    )(od, oc, gates, x2, b_gate.reshape(1, 2 * d), wbd, wbc, wo, norm2_w.reshape(1, d), wq)


def _pair_list():
    return [(a, b) for a in range(PEER_TOPK) for b in range(PEER_TOPK) if (a + 1) * (b + 1) <= PEER_TOPK]


def _extract_top(s, steps):
    rows = s.shape[0]
    idx = lax.broadcasted_iota(jnp.int32, s.shape, 0).astype(F32)
    rank = jnp.full(s.shape, float(steps), F32)
    vals = []
    for step in range(steps):
        m = jnp.max(s, axis=0, keepdims=True)
        first = jnp.min(jnp.where(s == m, idx, float(rows)), axis=0, keepdims=True)
        sel = idx == first
        rank = jnp.where(sel, float(step), rank)
        s = jnp.where(sel, -jnp.inf, s)
        vals.append(m)
    return vals, rank


def _route_body(qp_ref, kz_ref, r2_ref, bw_ref, ci_ref, ai_ref):
    nk = PEER_NKEYS
    pairs = _pair_list()
    st = lax.dot_general(kz_ref[0], qp_ref[...], _NT, preferred_element_type=F32)
    s1, s2 = st[:nk], st[nk:]
    v1, rank1 = _extract_top(s1, PEER_TOPK)
    v2, rank2 = _extract_top(s2, PEER_TOPK)
    cand = jnp.concatenate([v1[a] + v2[b] for a, b in pairs], axis=0)
    top = cand[0:1]
    _, crank = _extract_top(cand, PEER_TOPK)
    chosen = crank < float(PEER_TOPK)
    z = jnp.sum(jnp.where(chosen, jnp.exp(cand - top), 0.0), axis=0, keepdims=True)
    ci = jnp.zeros(rank1.shape, F32)
    row = 0
    for a in range(PEER_TOPK):
        nb = PEER_TOPK // (a + 1)
        cnt = jnp.sum(jnp.where(chosen[row:row + nb], 1.0, 0.0), axis=0, keepdims=True)
        ci = ci + jnp.where(rank1 == float(a), cnt, 0.0)
        row += nb
    r2_ref[0] = rank2.astype(BF16)
    bw_ref[0] = (jnp.exp(s2 - v2[0]) / z).astype(BF16)
    ci_ref[0] = ci
    ai_ref[0] = jnp.exp(s1 - v1[0])


def _peer_route(qp, kz, tt):
    n, w = qp.shape
    hw = w // PEER_HEADS
    spec = pl.BlockSpec((1, PEER_NKEYS, tt), lambda t, h: (h, 0, t))
    shp = lambda dt: jax.ShapeDtypeStruct((PEER_HEADS, PEER_NKEYS, n), dt)
    return pl.pallas_call(
        _route_body,
        grid=(n // tt, PEER_HEADS),
        in_specs=[pl.BlockSpec((tt, hw), lambda t, h: (t, h)),
                  pl.BlockSpec((1,) + kz.shape[1:], lambda t, h: (h, 0, 0))],
        out_specs=[spec, spec, spec, spec],
        out_shape=[shp(BF16), shp(BF16), shp(F32), shp(F32)],
        compiler_params=_params("parallel", "parallel"),
        name="peer_route",
    )(qp, kz)


_EG = 512
_OB = 512


def _row_tile(ref, h, i):
    row = ref[h, pl.ds(i, 1), :]
    return jnp.broadcast_to(row, (_BT, row.shape[1])).astype(BF16)


def _experts_body(xn_ref, u_ref, vt_ref, r2_ref, bw_ref, ci_ref, ai_ref, h_ref, fw_ref, y_ref, acc_sc, *, ec):
    c = pl.program_id(1)
    nk = PEER_NKEYS

    @pl.when(c == 0)
    def _():
        acc_sc[...] = jnp.zeros(acc_sc.shape, F32)

    xn = xn_ref[...]
    tokens = xn.shape[0]
    ngroups = ec // _EG

    def hidden(g):
        return lax.dot_general(u_ref[g * _EG:(g + 1) * _EG, :], xn, _NT, preferred_element_type=F32)

    nxt = hidden(0)
    for g in range(ngroups):
        hid = nxt
        if g + 1 < ngroups:
            nxt = hidden(g + 1)
        acts = []
        for gi in range(_EG // nk):
            i = (c * ec + g * _EG) // nk + gi
            w = None
            for h in range(PEER_HEADS):
                ci = _row_tile(ci_ref, h, i)[None]
                ai = _row_tile(ai_ref, h, i)[None]
                r2 = r2_ref[h].reshape(nk // _BT, _BT, tokens)
                bw = bw_ref[h].reshape(nk // _BT, _BT, tokens)
                term = jnp.where(r2 < ci, bw, jnp.zeros((), BF16)) * ai
                w = term if w is None else w + term
            hg = hid[gi * nk:(gi + 1) * nk]
            gelu = 0.5 * hg * (1.0 + lax.erf(hg * (2.0 ** -0.5)))
            acts.append((gelu * w.reshape(nk, tokens).astype(F32)).astype(BF16))
        act = jnp.concatenate(acts, axis=0)
        for lo in range(0, acc_sc.shape[0], _OB):
            acc_sc[lo:lo + _OB, :] += jnp.dot(vt_ref[lo:lo + _OB, g * _EG:(g + 1) * _EG], act,
                                              preferred_element_type=F32)

    @pl.when(c == pl.num_programs(1) - 1)
    def _():
        hfin = h_ref[...] + acc_sc[...].T
        y_ref[...] = _rms(hfin, fw_ref[...])


def _peer_experts(xn, u_bf16, vt_bf16, r2, bw, ci, ai, h1, final_w, tt, ec):
    n, d = xn.shape
    ne = u_bf16.shape[0]
    gate = pl.BlockSpec((PEER_HEADS, PEER_NKEYS, tt), lambda t, c: (0, 0, t))
    return pl.pallas_call(
        functools.partial(_experts_body, ec=ec),
        grid=(n // tt, ne // ec),
        in_specs=[pl.BlockSpec((tt, d), lambda t, c: (t, 0)),
                  pl.BlockSpec((ec, d), lambda t, c: (c, 0)),
                  pl.BlockSpec((d, ec), lambda t, c: (0, c)),
                  gate, gate, gate, gate,
                  pl.BlockSpec((tt, d), lambda t, c: (t, 0)),
                  pl.BlockSpec((1, d), lambda t, c: (0, 0))],
        out_specs=pl.BlockSpec((tt, d), lambda t, c: (t, 0)),
        out_shape=jax.ShapeDtypeStruct((n, d), F32),
        scratch_shapes=[pltpu.VMEM((d, tt), F32)],
        compiler_params=_params("parallel", "arbitrary"),
        name="peer_experts",
    )(xn, u_bf16, vt_bf16, r2, bw, ci, ai, h1, final_w.reshape(1, d))


def _peer_key_blocks(keys):
    z = jnp.zeros_like(keys[:, 0])
    top = jnp.concatenate([keys[:, 0], z], axis=-1)
    bot = jnp.concatenate([z, keys[:, 1]], axis=-1)
    return jnp.concatenate([top, bot], axis=1).astype(BF16)


def kernel(x, norm1_w, w_in, b_gate, diff_lq1, diff_lk1, diff_lq2, diff_lk2, diff_subln_w, chunk_rel_bias,
           w_branch_diff, w_branch_chunk, w_out, norm2_w, peer_wq, peer_keys, peer_u, peer_v, final_norm_w):
    batch, seq, d = x.shape
    depth = norm1_w.shape[0]
    n = batch * seq
    h = x.reshape(n, d)
    rel_idx = _rel_row_index()
    for l in range(depth):
        lam_init = 0.8 - 0.6 * math.exp(-0.3 * l)
        qd, kd, vdt, qc, kc, vc, gates = _in_proj(h, norm1_w[l], w_in[l].astype(BF16), tm=min(512, seq))
        lam_vecs = jnp.stack([diff_lq1[l], diff_lk1[l], diff_lq2[l], diff_lk2[l]])
        od = _diff_attn(qd, kd, vdt, lam_vecs, diff_subln_w[l], batch, seq, lam_init)
        oc = _chunk_attn(qc, kc, vc, chunk_rel_bias[l][:, rel_idx], batch, seq)
        h1, xn2, qp = _merge(od, oc, gates, h, b_gate[l], w_branch_diff[l].astype(BF16),
                             w_branch_chunk[l].astype(BF16), w_out[l].astype(BF16), norm2_w[l],
                             peer_wq[l].astype(BF16), tm=min(512, n))
        tt = min(512, n)
        r2, bw, ci, ai = _peer_route(qp, _peer_key_blocks(peer_keys[l]), tt)
        last = l == depth - 1
        fw = final_norm_w if last else jnp.ones((d,), F32)
        h = _peer_experts(xn2, peer_u[l].astype(BF16), peer_v[l].T.astype(BF16), r2, bw, ci, ai, h1, fw,
                          tt=tt, ec=2048)
        assert last, "the fused final RMSNorm assumes a single layer"
    return h.reshape(batch, seq, d)
```

```python
import functools
import math

import jax
import jax.numpy as jnp
import numpy as np
from jax import lax
from jax.experimental import pallas as pl
from jax.experimental.pallas import tpu as pltpu

F32 = jnp.float32
BF16 = jnp.bfloat16

EPS = 1e-6
CHUNK = 64
DIFF_HEADS = 8
DIFF_DH = 64
CH_HEADS = 8
CH_DH = 64
CH_LEFT = 8
REL_CLIP = 128
PEER_HEADS = 8
PEER_NKEYS = 128
PEER_TOPK = 16
NEG = -1e30
_BT = 16

VMEM_LIMIT = 56 * 1024 * 1024

_NT = (((1,), (1,)), ((), ()))


def _params(*sem):
    return pltpu.CompilerParams(dimension_semantics=sem, vmem_limit_bytes=VMEM_LIMIT)


def _rms(x, w):
    return x * lax.rsqrt(jnp.mean(x * x, axis=-1, keepdims=True) + EPS) * w


_QKV_WIDTHS = (1024, 1024, 1024, 512, 512, 512)
LOG2E = math.log2(math.e)
_Q_SCALE = (DIFF_DH ** -0.5 * LOG2E, None, None, CH_DH ** -0.5, None, None)
_VD = 2


def _inproj_body(x_ref, nw_ref, w_ref, wvt_ref, qd_ref, kd_ref, vdt_ref, qc_ref, kc_ref, vc_ref, g_ref):
    xn = _rms(x_ref[...], nw_ref[...]).astype(BF16)
    col = 0
    outs = (qd_ref, kd_ref, None, qc_ref, kc_ref, vc_ref)
    for ref, width, scale in zip(outs, _QKV_WIDTHS, _Q_SCALE):
        if ref is not None:
            acc = jnp.dot(xn, w_ref[:, col:col + width], preferred_element_type=F32)
            if scale is not None:
                acc = acc * scale
            ref[...] = acc.astype(BF16)
        col += width
    g_ref[...] = jnp.dot(xn, w_ref[:, col:], preferred_element_type=F32)
    vdt_ref[0] = lax.dot_general(wvt_ref[...], xn, _NT, preferred_element_type=F32).astype(BF16)


def _in_proj(x2, norm_w, w_in_bf16, tm):
    n, d = x2.shape
    cols = w_in_bf16.shape[1]
    vcol = sum(_QKV_WIDTHS[:_VD])
    wvt = w_in_bf16[:, vcol:vcol + _QKV_WIDTHS[_VD]].T
    row = lambda width: pl.BlockSpec((tm, width), lambda i: (i, 0))
    const = lambda shape: pl.BlockSpec(shape, lambda i: (0, 0))
    out_specs = [row(w) for w in _QKV_WIDTHS] + [row(2 * d)]
    out_shape = [jax.ShapeDtypeStruct((n, w), BF16) for w in _QKV_WIDTHS] + [jax.ShapeDtypeStruct((n, 2 * d), F32)]
    out_specs[_VD] = pl.BlockSpec((1, _QKV_WIDTHS[_VD], tm), lambda i: (i, 0, 0))
    out_shape[_VD] = jax.ShapeDtypeStruct((n // tm, _QKV_WIDTHS[_VD], tm), BF16)
    return pl.pallas_call(
        _inproj_body,
        grid=(n // tm,),
        in_specs=[row(d), const((1, d)), const((d, cols)), const(wvt.shape)],
        out_specs=out_specs,
        out_shape=out_shape,
        compiler_params=_params("parallel"),
        name="in_proj",
    )(x2, norm_w.reshape(1, d), w_in_bf16, wvt)


_QS = 256


def _diff_attn_body(q_ref, k_ref, vt_ref, lam_ref, subw_ref, o_ref, off_sc, diag_sc, sa_sc, sb_sc, *, t,
                    lam_init):
    h = pl.program_id(1)
    i = pl.program_id(2)
    slope = lax.bitcast_convert_type(jnp.full((1, 1), 126 - h, jnp.int32) << 23, F32) * LOG2E

    @pl.when(i == 0)
    def _():
        kpos = lax.broadcasted_iota(jnp.int32, (t, t), 0)
        qpos = lax.broadcasted_iota(jnp.int32, (t, t), 1)
        dist = (qpos - kpos).astype(F32)
        off_sc[...] = -slope * dist
        diag_sc[...] = jnp.where((kpos // CHUNK) <= (qpos // CHUNK), -slope * jnp.abs(dist), NEG)

    q = q_ref[...]
    lane = lax.broadcasted_iota(jnp.int32, q.shape, 1)
    zero = jnp.zeros_like(q)
    qz = jnp.concatenate([jnp.where(lane < DIFF_DH, q, zero), jnp.where(lane >= DIFF_DH, q, zero)], axis=0)

    def scores(j, bias_ref, s_ref):
        kb = k_ref[pl.ds(pl.multiple_of(j * t, t), t), :]
        s = lax.dot_general(kb, qz, _NT, preferred_element_type=F32)
        bias = bias_ref[...]
        s_ref[:, :t] = s[:, :t] + bias
        s_ref[:, t:] = s[:, t:] + bias

    def tile(j, s_ref, shift, carry):
        vtb = vt_ref[j]
        new = []
        for n in range(2 * t // _QS):
            lo = n * _QS
            m_old, l_old, acc = (c[:, lo:lo + _QS] for c in carry)
            m_new = jnp.maximum(m_old, jnp.max(s_ref[:, lo:lo + _QS], axis=0, keepdims=True) + shift)
            alpha = jnp.exp2(m_old - m_new)
            p = jnp.exp2(s_ref[:, lo:lo + _QS] - (m_new - shift))
            l_new = alpha * l_old + jnp.sum(p, axis=0, keepdims=True)
            acc = alpha * acc + jnp.dot(vtb, p.astype(BF16), preferred_element_type=F32)
            new.append((m_new, l_new, acc))
        return tuple(jnp.concatenate(parts, axis=1) for parts in zip(*new))

    def off_tile(o, s_ref, carry):
        return tile(o, s_ref, -slope * ((i - o) * t).astype(F32), carry)

    def pair(p, carry):
        o = 2 * p
        scores(o + 1, off_sc, sa_sc)
        carry = off_tile(o, sb_sc, carry)
        scores(jnp.minimum(o + 2, i - 1), off_sc, sb_sc)
        return off_tile(o + 1, sa_sc, carry)

    init = (jnp.full((1, 2 * t), NEG, F32), jnp.zeros((1, 2 * t), F32), jnp.zeros((2 * DIFF_DH, 2 * t), F32))
    scores(i, diag_sc, sa_sc)
    scores(0, off_sc, sb_sc)
    carry = tile(i, sa_sc, jnp.zeros((1, 1), F32), init)
    carry = lax.fori_loop(0, i // 2, pair, carry)
    _, l, acc = lax.cond(i % 2 == 1, functools.partial(off_tile, i - 1, sb_sc), lambda c: c, carry)

    o = acc / l
    lv = lam_ref[...]
    lam = (jnp.exp(jnp.sum(lv[0:1] * lv[1:2], axis=-1, keepdims=True))
           - jnp.exp(jnp.sum(lv[2:3] * lv[3:4], axis=-1, keepdims=True)) + lam_init)
    o = o[:, :t] - lam * o[:, t:]
    y = o * lax.rsqrt(jnp.mean(o * o, axis=0, keepdims=True) + EPS) * (subw_ref[...] * (1.0 - lam_init))
    o_ref[...] = y.T.astype(BF16)


def _diff_attn(qd, kd, vdt, lam_vecs, subln_w, batch, seq, lam_init):
    n = qd.shape[0]
    t = vdt.shape[2]
    nq = seq // t
    hw = 2 * DIFF_DH
    return pl.pallas_call(
        functools.partial(_diff_attn_body, t=t, lam_init=lam_init),
        grid=(batch, DIFF_HEADS, nq),
        in_specs=[
            pl.BlockSpec((t, hw), lambda b, h, i: (b * nq + i, h)),
            pl.BlockSpec((seq, hw), lambda b, h, i: (b, h)),
            pl.BlockSpec((nq, hw, t), lambda b, h, i: (b, h, 0)),
            pl.BlockSpec((4, DIFF_DH), lambda b, h, i: (0, 0)),
            pl.BlockSpec((hw, 1), lambda b, h, i: (0, 0)),
        ],
        out_specs=pl.BlockSpec((t, hw), lambda b, h, i: (b * nq + i, h)),
        out_shape=jax.ShapeDtypeStruct((n, DIFF_HEADS * hw), BF16),
        scratch_shapes=[pltpu.VMEM((t, t), F32), pltpu.VMEM((t, t), F32),
                        pltpu.VMEM((t, 2 * t), F32), pltpu.VMEM((t, 2 * t), F32)],
        compiler_params=_params("parallel", "parallel", "arbitrary"),
        name="diff_attn",
    )(qd, kd, vdt, lam_vecs, subln_w.reshape(hw, 1))


_CT = 256
_CWIN = 3 * _CT
_FW = 1024


def _rel_row_index():
    lanes = np.arange(_FW)
    u = np.where(lanes < _CWIN, lanes, lanes - _FW)
    rel = CH_LEFT * CHUNK - u
    return np.clip(rel, -REL_CLIP, REL_CLIP) + REL_CLIP


def _chunk_attn_body(q_ref, k0_ref, k1_ref, k2_ref, v0_ref, v1_ref, v2_ref, f_ref, o_ref, bias_sc):
    i = pl.program_id(1)

    @pl.when((pl.program_id(0) == 0) & (i == 0))
    def _():
        r = lax.broadcasted_iota(jnp.int32, (_CT, _CWIN), 0) // CHUNK
        c = lax.broadcasted_iota(jnp.int32, (_CT, _CWIN), 1) // CHUNK
        allowed = (c >= r) & (c <= r + CH_LEFT)
        for h in range(CH_HEADS):
            row = jnp.broadcast_to(f_ref[h:h + 1, :], (_CT, _FW))
            toeplitz = pltpu.roll(row, 0, 1, stride=1, stride_axis=0)
            bias_sc[h] = jnp.where(allowed, toeplitz[:, :_CWIN], NEG)

    q = q_ref[...]
    ks = (k0_ref[...], k1_ref[...], k2_ref[...])
    vs = (v0_ref[...], v1_ref[...], v2_ref[...])
    pen = [jnp.where(i - 2 + d >= 0, 0.0, NEG).astype(F32) for d in range(2)] + [None]
    outs = []
    for h in range(CH_HEADS):
        sl = slice(h * CH_DH, (h + 1) * CH_DH)
        qh = q[:, sl]
        parts = []
        for d in range(3):
            sd = lax.dot_general(qh, ks[d][:, sl], _NT, preferred_element_type=F32)
            parts.append(sd if pen[d] is None else sd + pen[d])
        s = jnp.concatenate(parts, axis=1) + bias_sc[h]
        m = jnp.max(s, axis=-1, keepdims=True)
        p = jnp.exp(s - m)
        l = jnp.sum(p, axis=-1, keepdims=True)
        pb = p.astype(BF16)
        o = jnp.dot(pb[:, :_CT], vs[0][:, sl], preferred_element_type=F32)
        o += jnp.dot(pb[:, _CT:2 * _CT], vs[1][:, sl], preferred_element_type=F32)
        o += jnp.dot(pb[:, 2 * _CT:], vs[2][:, sl], preferred_element_type=F32)
        outs.append(o / l)
    o_ref[...] = jnp.concatenate(outs, axis=1).astype(BF16)


def _chunk_attn(qc, kc, vc, rel_rows, batch, seq):
    n, w = qc.shape
    nq = seq // _CT
    win = lambda d: pl.BlockSpec((_CT, w), lambda b, i: (b * nq + jnp.maximum(i - 2 + d, 0), 0))
    return pl.pallas_call(
        _chunk_attn_body,
        grid=(batch, nq),
        in_specs=[pl.BlockSpec((_CT, w), lambda b, i: (b * nq + i, 0)),
                  win(0), win(1), win(2), win(0), win(1), win(2),
                  pl.BlockSpec((CH_HEADS, _FW), lambda b, i: (0, 0))],
        out_specs=pl.BlockSpec((_CT, w), lambda b, i: (b * nq + i, 0)),
        out_shape=jax.ShapeDtypeStruct((n, w), BF16),
        scratch_shapes=[pltpu.VMEM((CH_HEADS, _CT, _CWIN), F32)],
        compiler_params=_params("arbitrary", "arbitrary"),
        name="chunk_attn",
    )(qc, kc, kc, kc, vc, vc, vc, rel_rows)


def _merge_body(od_ref, oc_ref, g_ref, x_ref, bg_ref, wbd_ref, wbc_ref, wo_ref, n2_ref, wq_ref,
                h_ref, xn_ref, qp_ref):
    d = x_ref.shape[1]
    g = g_ref[...] + bg_ref[...]
    pd = jnp.dot(od_ref[...], wbd_ref[...], preferred_element_type=F32)
    pc = jnp.dot(oc_ref[...], wbc_ref[...], preferred_element_type=F32)
    merged = jax.nn.sigmoid(g[:, :d]) * pd + jax.nn.sigmoid(g[:, d:]) * pc
    h1 = x_ref[...] + jnp.dot(merged.astype(BF16), wo_ref[...], preferred_element_type=F32)
    h_ref[...] = h1
    xn = _rms(h1, n2_ref[...]).astype(BF16)
    xn_ref[...] = xn
    qp_ref[...] = jnp.dot(xn, wq_ref[...], preferred_element_type=F32).astype(BF16)


def _merge(od, oc, gates, x2, b_gate, wbd, wbc, wo, norm2_w, wq, tm):
    n, d = x2.shape
    row = lambda width: pl.BlockSpec((tm, width), lambda i: (i, 0))
    const = lambda shape: pl.BlockSpec(shape, lambda i: (0, 0))
    return pl.pallas_call(
        _merge_body,
        grid=(n // tm,),
        in_specs=[row(od.shape[1]), row(oc.shape[1]), row(2 * d), row(d), const((1, 2 * d)),
                  const(wbd.shape), const(wbc.shape), const(wo.shape), const((1, d)), const(wq.shape)],
        out_specs=[row(d), row(d), row(wq.shape[1])],
        out_shape=[jax.ShapeDtypeStruct((n, d), F32), jax.ShapeDtypeStruct((n, d), BF16),
                   jax.ShapeDtypeStruct((n, wq.shape[1]), BF16)],
        compiler_params=_params("parallel"),
        name="merge",
    )(od, oc, gates, x2, b_gate.reshape(1, 2 * d), wbd, wbc, wo, norm2_w.reshape(1, d), wq)


def _pair_list():
    return [(a, b) for a in range(PEER_TOPK) for b in range(PEER_TOPK) if (a + 1) * (b + 1) <= PEER_TOPK]


def _extract_top(s, steps):
    rows = s.shape[0]
    idx = lax.broadcasted_iota(jnp.int32, s.shape, 0).astype(F32)
    rank = jnp.full(s.shape, float(steps), F32)
    vals = []
    for step in range(steps):
        m = jnp.max(s, axis=0, keepdims=True)
        first = jnp.min(jnp.where(s == m, idx, float(rows)), axis=0, keepdims=True)
        sel = idx == first
        rank = jnp.where(sel, float(step), rank)
        s = jnp.where(sel, -jnp.inf, s)
        vals.append(m)
    return vals, rank


def _extract_top_untied(s, steps):
    rank = jnp.full(s.shape, float(steps), F32)
    vals = []
    for step in range(steps):
        m = jnp.max(s, axis=0, keepdims=True)
        sel = s == m
        rank = jnp.where(sel, float(step), rank)
        s = jnp.where(sel, -jnp.inf, s)
        vals.append(m)
    removed = jnp.sum(jnp.where(rank < float(steps), 1.0, 0.0), axis=0, keepdims=True)
    return vals, rank, jnp.max(jnp.abs(removed - float(steps)))


def _route_outputs(s1, s2, v1, rank1, v2, rank2, r2_ref, bw_ref, ci_ref, ai_ref):
    pairs = _pair_list()
    cand = jnp.concatenate([v1[a] + v2[b] for a, b in pairs], axis=0)
    top = cand[0:1]
    _, crank = _extract_top(cand, PEER_TOPK)
    chosen = crank < float(PEER_TOPK)
    z = jnp.sum(jnp.where(chosen, jnp.exp(cand - top), 0.0), axis=0, keepdims=True)
    ci = jnp.zeros(rank1.shape, F32)
    row = 0
    for a in range(PEER_TOPK):
        nb = PEER_TOPK // (a + 1)
        cnt = jnp.sum(jnp.where(chosen[row:row + nb], 1.0, 0.0), axis=0, keepdims=True)
        ci = jnp.where(rank1 == float(a), cnt, ci)
        row += nb
    r2_ref[0] = rank2.astype(BF16)
    bw_ref[0] = (jnp.exp(s2 - v2[0]) / z).astype(BF16)
    ci_ref[0] = ci
    ai_ref[0] = jnp.exp(s1 - v1[0])


def _route_body(qp_ref, kz_ref, r2_ref, bw_ref, ci_ref, ai_ref):
    nk = PEER_NKEYS
    st = lax.dot_general(kz_ref[0], qp_ref[...], _NT, preferred_element_type=F32)
    s1, s2 = st[:nk], st[nk:]
    outs = (r2_ref, bw_ref, ci_ref, ai_ref)
    v1, rank1, tied1 = _extract_top_untied(s1, PEER_TOPK)
    v2, rank2, tied2 = _extract_top_untied(s2, PEER_TOPK)
    _route_outputs(s1, s2, v1, rank1, v2, rank2, *outs)

    @pl.when(jnp.maximum(tied1, tied2) > 0.0)
    def _():
        _route_outputs(s1, s2, *_extract_top(s1, PEER_TOPK), *_extract_top(s2, PEER_TOPK), *outs)


def _peer_route(qp, kz, tt):
    n, w = qp.shape
    hw = w // PEER_HEADS
    spec = pl.BlockSpec((1, PEER_NKEYS, tt), lambda t, h: (h, 0, t))
    shp = lambda dt: jax.ShapeDtypeStruct((PEER_HEADS, PEER_NKEYS, n), dt)
    return pl.pallas_call(
        _route_body,
        grid=(n // tt, PEER_HEADS),
        in_specs=[pl.BlockSpec((tt, hw), lambda t, h: (t, h)),
                  pl.BlockSpec((1,) + kz.shape[1:], lambda t, h: (h, 0, 0))],
        out_specs=[spec, spec, spec, spec],
        out_shape=[shp(BF16), shp(BF16), shp(F32), shp(F32)],
        compiler_params=_params("parallel", "parallel"),
        name="peer_route",
    )(qp, kz)


_EG = 512
_OB = 512


def _row_tile(ref, h, i):
    row = ref[h, pl.ds(i, 1), :]
    return jnp.broadcast_to(row, (_BT, row.shape[1])).astype(BF16)


def _experts_body(xn_ref, u_ref, vt_ref, r2_ref, bw_ref, ci_ref, ai_ref, h_ref, fw_ref, y_ref, acc_sc, *, ec):
    c = pl.program_id(1)
    nk = PEER_NKEYS

    @pl.when(c == 0)
    def _():
        acc_sc[...] = jnp.zeros(acc_sc.shape, F32)

    xn = xn_ref[...]
    tokens = xn.shape[0]
    ngroups = ec // _EG

    def hidden(g):
        return lax.dot_general(u_ref[g * _EG:(g + 1) * _EG, :], xn, _NT, preferred_element_type=F32)

    nxt = hidden(0)
    for g in range(ngroups):
        hid = nxt
        if g + 1 < ngroups:
            nxt = hidden(g + 1)
        acts = []
        for gi in range(_EG // nk):
            i = (c * ec + g * _EG) // nk + gi
            w = None
            for h in range(PEER_HEADS):
                ci = _row_tile(ci_ref, h, i)[None]
                ai = _row_tile(ai_ref, h, i)[None]
                r2 = r2_ref[h].reshape(nk // _BT, _BT, tokens)
                bw = bw_ref[h].reshape(nk // _BT, _BT, tokens)
                term = jnp.where(r2 < ci, bw, jnp.zeros((), BF16)) * ai
                w = term if w is None else w + term
            hg = hid[gi * nk:(gi + 1) * nk]
            gelu = 0.5 * hg * (1.0 + lax.erf(hg * (2.0 ** -0.5)))
            acts.append((gelu * w.reshape(nk, tokens).astype(F32)).astype(BF16))
        act = jnp.concatenate(acts, axis=0)
        for lo in range(0, acc_sc.shape[0], _OB):
            acc_sc[lo:lo + _OB, :] += jnp.dot(vt_ref[lo:lo + _OB, g * _EG:(g + 1) * _EG], act,
                                              preferred_element_type=F32)

    @pl.when(c == pl.num_programs(1) - 1)
    def _():
        hfin = h_ref[...] + acc_sc[...].T
        y_ref[...] = _rms(hfin, fw_ref[...])


def _peer_experts(xn, u_bf16, vt_bf16, r2, bw, ci, ai, h1, final_w, tt, ec):
    n, d = xn.shape
    ne = u_bf16.shape[0]
    gate = pl.BlockSpec((PEER_HEADS, PEER_NKEYS, tt), lambda t, c: (0, 0, t))
    return pl.pallas_call(
        functools.partial(_experts_body, ec=ec),
        grid=(n // tt, ne // ec),
        in_specs=[pl.BlockSpec((tt, d), lambda t, c: (t, 0)),
                  pl.BlockSpec((ec, d), lambda t, c: (c, 0)),
                  pl.BlockSpec((d, ec), lambda t, c: (0, c)),
                  gate, gate, gate, gate,
                  pl.BlockSpec((tt, d), lambda t, c: (t, 0)),
                  pl.BlockSpec((1, d), lambda t, c: (0, 0))],
        out_specs=pl.BlockSpec((tt, d), lambda t, c: (t, 0)),
        out_shape=jax.ShapeDtypeStruct((n, d), F32),
        scratch_shapes=[pltpu.VMEM((d, tt), F32)],
        compiler_params=_params("parallel", "arbitrary"),
        name="peer_experts",
    )(xn, u_bf16, vt_bf16, r2, bw, ci, ai, h1, final_w.reshape(1, d))


def _peer_key_blocks(keys):
    z = jnp.zeros_like(keys[:, 0])
    top = jnp.concatenate([keys[:, 0], z], axis=-1)
    bot = jnp.concatenate([z, keys[:, 1]], axis=-1)
    return jnp.concatenate([top, bot], axis=1).astype(BF16)


def kernel(x, norm1_w, w_in, b_gate, diff_lq1, diff_lk1, diff_lq2, diff_lk2, diff_subln_w, chunk_rel_bias,
           w_branch_diff, w_branch_chunk, w_out, norm2_w, peer_wq, peer_keys, peer_u, peer_v, final_norm_w):
    batch, seq, d = x.shape
    depth = norm1_w.shape[0]
    n = batch * seq
    h = x.reshape(n, d)
    rel_idx = _rel_row_index()
    for l in range(depth):
        lam_init = 0.8 - 0.6 * math.exp(-0.3 * l)
        qd, kd, vdt, qc, kc, vc, gates = _in_proj(h, norm1_w[l], w_in[l].astype(BF16), tm=min(512, seq))
        lam_vecs = jnp.stack([diff_lq1[l], diff_lk1[l], diff_lq2[l], diff_lk2[l]])
        od = _diff_attn(qd, kd, vdt, lam_vecs, diff_subln_w[l], batch, seq, lam_init)
        oc = _chunk_attn(qc, kc, vc, chunk_rel_bias[l][:, rel_idx], batch, seq)
        h1, xn2, qp = _merge(od, oc, gates, h, b_gate[l], w_branch_diff[l].astype(BF16),
                             w_branch_chunk[l].astype(BF16), w_out[l].astype(BF16), norm2_w[l],
                             peer_wq[l].astype(BF16), tm=min(512, n))
        tt = min(512, n)
        r2, bw, ci, ai = _peer_route(qp, _peer_key_blocks(peer_keys[l]), tt)
        last = l == depth - 1
        fw = final_norm_w if last else jnp.ones((d,), F32)
        h = _peer_experts(xn2, peer_u[l].astype(BF16), peer_v[l].T.astype(BF16), r2, bw, ci, ai, h1, fw,
                          tt=tt, ec=2048)
        assert last, "the fused final RMSNorm assumes a single layer"
    return h.reshape(batch, seq, d)
```

```python
import functools
import math

import jax
import jax.numpy as jnp
import numpy as np
from jax import lax
from jax.experimental import pallas as pl
from jax.experimental.pallas import tpu as pltpu

F32 = jnp.float32
BF16 = jnp.bfloat16

EPS = 1e-6
CHUNK = 64
DIFF_HEADS = 8
DIFF_DH = 64
CH_HEADS = 8
CH_DH = 64
CH_LEFT = 8
REL_CLIP = 128
PEER_HEADS = 8
PEER_NKEYS = 128
PEER_TOPK = 16
NEG = -1e30
_BT = 16

VMEM_LIMIT = 56 * 1024 * 1024

_NT = (((1,), (1,)), ((), ()))


def _params(*sem):
    return pltpu.CompilerParams(dimension_semantics=sem, vmem_limit_bytes=VMEM_LIMIT)


def _rms(x, w):
    return x * lax.rsqrt(jnp.mean(x * x, axis=-1, keepdims=True) + EPS) * w


_QKV_WIDTHS = (1024, 1024, 1024, 512, 512, 512)
LOG2E = math.log2(math.e)
_Q_SCALE = (DIFF_DH ** -0.5 * LOG2E, None, None, CH_DH ** -0.5 * LOG2E, None, None)
_TRANSPOSED = (2, 5)


def _inproj_body(x_ref, nw_ref, w_ref, wvt_ref, qd_ref, kd_ref, vdt_ref, qc_ref, kc_ref, vct_ref, g_ref):
    xn = _rms(x_ref[...], nw_ref[...]).astype(BF16)
    col = 0
    outs = (qd_ref, kd_ref, None, qc_ref, kc_ref, None)
    for ref, width, scale in zip(outs, _QKV_WIDTHS, _Q_SCALE):
        if ref is not None:
            acc = jnp.dot(xn, w_ref[:, col:col + width], preferred_element_type=F32)
            if scale is not None:
                acc = acc * scale
            ref[...] = acc.astype(BF16)
        col += width
    g_ref[...] = jnp.dot(xn, w_ref[:, col:], preferred_element_type=F32)
    vt = lax.dot_general(wvt_ref[...], xn, _NT, preferred_element_type=F32).astype(BF16)
    nd = vdt_ref.shape[1]
    vdt_ref[0] = vt[:nd]
    vct_ref[0] = vt[nd:]


def _in_proj(x2, norm_w, w_in_bf16, tm):
    n, d = x2.shape
    cols = w_in_bf16.shape[1]
    starts = np.cumsum((0,) + _QKV_WIDTHS)
    wvt = jnp.concatenate([w_in_bf16[:, starts[k]:starts[k + 1]].T for k in _TRANSPOSED], axis=0)
    row = lambda width: pl.BlockSpec((tm, width), lambda i: (i, 0))
    const = lambda shape: pl.BlockSpec(shape, lambda i: (0, 0))
    out_specs = [row(w) for w in _QKV_WIDTHS] + [row(2 * d)]
    out_shape = [jax.ShapeDtypeStruct((n, w), BF16) for w in _QKV_WIDTHS] + [jax.ShapeDtypeStruct((n, 2 * d), F32)]
    for k in _TRANSPOSED:
        out_specs[k] = pl.BlockSpec((1, _QKV_WIDTHS[k], tm), lambda i: (i, 0, 0))
        out_shape[k] = jax.ShapeDtypeStruct((n // tm, _QKV_WIDTHS[k], tm), BF16)
    return pl.pallas_call(
        _inproj_body,
        grid=(n // tm,),
        in_specs=[row(d), const((1, d)), const((d, cols)), const(wvt.shape)],
        out_specs=out_specs,
        out_shape=out_shape,
        compiler_params=_params("parallel"),
        name="in_proj",
    )(x2, norm_w.reshape(1, d), w_in_bf16, wvt)


_QS = 256


def _diff_attn_body(q_ref, k_ref, vt_ref, lam_ref, subw_ref, o_ref, off_sc, diag_sc, sa_sc, sb_sc, acc_sc, *,
                    t, lam_init):
    h = pl.program_id(1)
    i = pl.program_id(2)
    slope = lax.bitcast_convert_type(jnp.full((1, 1), 126 - h, jnp.int32) << 23, F32) * LOG2E

    @pl.when(i == 0)
    def _():
        kpos = lax.broadcasted_iota(jnp.int32, (t, t), 0)
        qpos = lax.broadcasted_iota(jnp.int32, (t, t), 1)
        dist = (qpos - kpos).astype(F32)
        off_sc[...] = -slope * dist
        diag_sc[...] = jnp.where((kpos // CHUNK) <= (qpos // CHUNK), -slope * jnp.abs(dist), NEG)

    q = q_ref[...]
    lane = lax.broadcasted_iota(jnp.int32, q.shape, 1)
    zero = jnp.zeros_like(q)
    qz = jnp.concatenate([jnp.where(lane < DIFF_DH, q, zero), jnp.where(lane >= DIFF_DH, q, zero)], axis=0)

    def scores(j, bias_ref, s_ref):
        kb = k_ref[pl.ds(pl.multiple_of(j * t, t), t), :]
        s = lax.dot_general(kb, qz, _NT, preferred_element_type=F32)
        bias = bias_ref[...]
        s_ref[:, :t] = s[:, :t] + bias
        s_ref[:, t:] = s[:, t:] + bias

    def tile(j, s_ref, shift, carry):
        vtb = vt_ref[j]
        new = []
        for n in range(2 * t // _QS):
            lo = n * _QS
            m_old, l_old = (c[:, lo:lo + _QS] for c in carry)
            m_new = jnp.maximum(m_old, jnp.max(s_ref[:, lo:lo + _QS], axis=0, keepdims=True) + shift)
            alpha = jnp.exp2(m_old - m_new)
            p = jnp.exp2(s_ref[:, lo:lo + _QS] - (m_new - shift))
            l_new = alpha * l_old + jnp.sum(p, axis=0, keepdims=True)
            acc_sc[:, lo:lo + _QS] = (alpha * acc_sc[:, lo:lo + _QS]
                                      + jnp.dot(vtb, p.astype(BF16), preferred_element_type=F32))
            new.append((m_new, l_new))
        return tuple(jnp.concatenate(parts, axis=1) for parts in zip(*new))

    def off_tile(o, s_ref, carry):
        return tile(o, s_ref, -slope * ((i - o) * t).astype(F32), carry)

    def pair(p, carry):
        o = 2 * p
        scores(o + 1, off_sc, sa_sc)
        carry = off_tile(o, sb_sc, carry)
        scores(jnp.minimum(o + 2, i - 1), off_sc, sb_sc)
        return off_tile(o + 1, sa_sc, carry)

    init = (jnp.full((1, 2 * t), NEG, F32), jnp.zeros((1, 2 * t), F32))
    acc_sc[...] = jnp.zeros(acc_sc.shape, F32)
    scores(i, diag_sc, sa_sc)
    scores(0, off_sc, sb_sc)
    carry = tile(i, sa_sc, jnp.zeros((1, 1), F32), init)
    carry = lax.fori_loop(0, i // 2, pair, carry)
    _, l = lax.cond(i % 2 == 1, functools.partial(off_tile, i - 1, sb_sc), lambda c: c, carry)
    acc = acc_sc[...]

    o = acc / l
    lv = lam_ref[...]
    lam = (jnp.exp(jnp.sum(lv[0:1] * lv[1:2], axis=-1, keepdims=True))
           - jnp.exp(jnp.sum(lv[2:3] * lv[3:4], axis=-1, keepdims=True)) + lam_init)
    o = o[:, :t] - lam * o[:, t:]
    y = o * lax.rsqrt(jnp.mean(o * o, axis=0, keepdims=True) + EPS) * (subw_ref[...] * (1.0 - lam_init))
    o_ref[...] = y.T.astype(BF16)


def _diff_attn(qd, kd, vdt, lam_vecs, subln_w, batch, seq, lam_init):
    n = qd.shape[0]
    t = vdt.shape[2]
    nq = seq // t
    hw = 2 * DIFF_DH
    return pl.pallas_call(
        functools.partial(_diff_attn_body, t=t, lam_init=lam_init),
        grid=(batch, DIFF_HEADS, nq),
        in_specs=[
            pl.BlockSpec((t, hw), lambda b, h, i: (b * nq + i, h)),
            pl.BlockSpec((seq, hw), lambda b, h, i: (b, h)),
            pl.BlockSpec((nq, hw, t), lambda b, h, i: (b, h, 0)),
            pl.BlockSpec((4, DIFF_DH), lambda b, h, i: (0, 0)),
            pl.BlockSpec((hw, 1), lambda b, h, i: (0, 0)),
        ],
        out_specs=pl.BlockSpec((t, hw), lambda b, h, i: (b * nq + i, h)),
        out_shape=jax.ShapeDtypeStruct((n, DIFF_HEADS * hw), BF16),
        scratch_shapes=[pltpu.VMEM((t, t), F32), pltpu.VMEM((t, t), F32),
                        pltpu.VMEM((t, 2 * t), F32), pltpu.VMEM((t, 2 * t), F32), pltpu.VMEM((hw, 2 * t), F32)],
        compiler_params=_params("parallel", "parallel", "arbitrary"),
        name="diff_attn",
    )(qd, kd, vdt, lam_vecs, subln_w.reshape(hw, 1))


_CT = 256
_CWIN = 3 * _CT
_FW = 1024
_HP = 2 * CH_DH


def _rel_row_index():
    u = (-np.arange(_FW)) % _FW
    u = np.where(u < _CWIN, u, u - _FW)
    rel = CH_LEFT * CHUNK - u
    return np.clip(rel, -REL_CLIP, REL_CLIP) + REL_CLIP


def _chunk_attn_body(q_ref, k0_ref, k1_ref, k2_ref, v0_ref, v1_ref, v2_ref, f_ref, o_ref, bias_sc):
    i = pl.program_id(1)

    @pl.when((pl.program_id(0) == 0) & (i == 0))
    def _():
        c = lax.broadcasted_iota(jnp.int32, (_CWIN, _CT), 0) // CHUNK
        r = lax.broadcasted_iota(jnp.int32, (_CWIN, _CT), 1) // CHUNK
        allowed = (c >= r) & (c <= r + CH_LEFT)
        for h in range(CH_HEADS):
            row = jnp.broadcast_to(f_ref[h:h + 1, :], (_CWIN, _FW))
            toeplitz = pltpu.roll(row, 0, 1, stride=1, stride_axis=0)
            bias_sc[h] = jnp.where(allowed, toeplitz[:, :_CT] * LOG2E, NEG)

    def attend(clamped):
        q = q_ref[...]
        lane = lax.broadcasted_iota(jnp.int32, (_CT, _HP), 1)
        zero = jnp.zeros((_CT, _HP), BF16)
        outs = []
        for hp in range(CH_HEADS // 2):
            sl = slice(hp * _HP, (hp + 1) * _HP)
            qp = q[:, sl]
            qz = jnp.concatenate([jnp.where(lane < CH_DH, qp, zero), jnp.where(lane >= CH_DH, qp, zero)], axis=0)
            kp = jnp.concatenate([k0_ref[:, sl], k1_ref[:, sl], k2_ref[:, sl]], axis=0)
            s = lax.dot_general(kp, qz, _NT, preferred_element_type=F32)
            s = jnp.concatenate([s[:, :_CT] + bias_sc[2 * hp], s[:, _CT:] + bias_sc[2 * hp + 1]], axis=1)
            if clamped:
                pen = [jnp.where(i - 2 + d >= 0, 0.0, NEG).astype(F32) for d in range(2)]
                s = jnp.concatenate([s[:_CT] + pen[0], s[_CT:2 * _CT] + pen[1], s[2 * _CT:]], axis=0)
            m = jnp.max(s, axis=0, keepdims=True)
            p = jnp.exp2(s - m)
            l = jnp.sum(p, axis=0, keepdims=True)
            vt = jnp.concatenate([v0_ref[0, sl, :], v1_ref[0, sl, :], v2_ref[0, sl, :]], axis=1)
            o = jnp.dot(vt, p.astype(BF16), preferred_element_type=F32) / l
            outs += [o[:CH_DH, :_CT], o[CH_DH:, _CT:]]
        o_ref[...] = jnp.concatenate(outs, axis=0).T.astype(BF16)

    @pl.when(i < 2)
    def _():
        attend(True)

    @pl.when(i >= 2)
    def _():
        attend(False)


def _chunk_attn(qc, kc, vct, rel_rows, batch, seq):
    n, w = qc.shape
    nq = seq // _CT
    per_tile = vct.shape[2] // _CT
    block = lambda b, i, d: b * nq + jnp.maximum(i - 2 + d, 0)
    kwin = lambda d: pl.BlockSpec((_CT, w), lambda b, i: (block(b, i, d), 0))
    vwin = lambda d: pl.BlockSpec((1, w, _CT), lambda b, i: (block(b, i, d) // per_tile, 0, block(b, i, d) % per_tile))
    return pl.pallas_call(
        _chunk_attn_body,
        grid=(batch, nq),
        in_specs=[pl.BlockSpec((_CT, w), lambda b, i: (b * nq + i, 0)),
                  kwin(0), kwin(1), kwin(2), vwin(0), vwin(1), vwin(2),
                  pl.BlockSpec((CH_HEADS, _FW), lambda b, i: (0, 0))],
        out_specs=pl.BlockSpec((_CT, w), lambda b, i: (b * nq + i, 0)),
        out_shape=jax.ShapeDtypeStruct((n, w), BF16),
        scratch_shapes=[pltpu.VMEM((CH_HEADS, _CWIN, _CT), F32)],
        compiler_params=_params("arbitrary", "arbitrary"),
        name="chunk_attn",
    )(qc, kc, kc, kc, vct, vct, vct, rel_rows)


def _merge_body(od_ref, oc_ref, g_ref, x_ref, bg_ref, wbd_ref, wbc_ref, wo_ref, n2_ref, wq_ref,
                h_ref, xn_ref, qp_ref):
    d = x_ref.shape[1]
    g = g_ref[...] + bg_ref[...]
    pd = jnp.dot(od_ref[...], wbd_ref[...], preferred_element_type=F32)
    pc = jnp.dot(oc_ref[...], wbc_ref[...], preferred_element_type=F32)
    merged = jax.nn.sigmoid(g[:, :d]) * pd + jax.nn.sigmoid(g[:, d:]) * pc
    h1 = x_ref[...] + jnp.dot(merged.astype(BF16), wo_ref[...], preferred_element_type=F32)
    h_ref[...] = h1
    xn = _rms(h1, n2_ref[...]).astype(BF16)
    xn_ref[...] = xn
    qp_ref[...] = jnp.dot(xn, wq_ref[...], preferred_element_type=F32).astype(BF16)


def _merge(od, oc, gates, x2, b_gate, wbd, wbc, wo, norm2_w, wq, tm):
    n, d = x2.shape
    row = lambda width: pl.BlockSpec((tm, width), lambda i: (i, 0))
    const = lambda shape: pl.BlockSpec(shape, lambda i: (0, 0))
    return pl.pallas_call(
        _merge_body,
        grid=(n // tm,),
        in_specs=[row(od.shape[1]), row(oc.shape[1]), row(2 * d), row(d), const((1, 2 * d)),
                  const(wbd.shape), const(wbc.shape), const(wo.shape), const((1, d)), const(wq.shape)],
        out_specs=[row(d), row(d), row(wq.shape[1])],
        out_shape=[jax.ShapeDtypeStruct((n, d), F32), jax.ShapeDtypeStruct((n, d), BF16),
                   jax.ShapeDtypeStruct((n, wq.shape[1]), BF16)],
        compiler_params=_params("parallel"),
        name="merge",
    )(od, oc, gates, x2, b_gate.reshape(1, 2 * d), wbd, wbc, wo, norm2_w.reshape(1, d), wq)


def _pair_list():
    return [(a, b) for a in range(PEER_TOPK) for b in range(PEER_TOPK) if (a + 1) * (b + 1) <= PEER_TOPK]


def _extract_top(s, steps):
    rows = s.shape[0]
    idx = lax.broadcasted_iota(jnp.int32, s.shape, 0).astype(F32)
    rank = jnp.full(s.shape, float(steps), F32)
    vals = []
    for step in range(steps):
        m = jnp.max(s, axis=0, keepdims=True)
        first = jnp.min(jnp.where(s == m, idx, float(rows)), axis=0, keepdims=True)
        sel = idx == first
        rank = jnp.where(sel, float(step), rank)
        s = jnp.where(sel, -jnp.inf, s)
        vals.append(m)
    return vals, rank


def _extract_top_untied(s, steps):
    rank = jnp.full(s.shape, float(steps), F32)
    vals = []
    for step in range(steps):
        m = jnp.max(s, axis=0, keepdims=True)
        sel = s == m
        rank = jnp.where(sel, float(step), rank)
        s = jnp.where(sel, -jnp.inf, s)
        vals.append(m)
    removed = jnp.sum(jnp.where(rank < float(steps), 1.0, 0.0), axis=0, keepdims=True)
    return vals, rank, jnp.max(jnp.abs(removed - float(steps)))


def _route_outputs(s1, s2, v1, rank1, v2, rank2, r2_ref, bw_ref, ci_ref, ai_ref):
    pairs = _pair_list()
    cand = jnp.concatenate([v1[a] + v2[b] for a, b in pairs], axis=0)
    top = cand[0:1]
    _, crank = _extract_top(cand, PEER_TOPK)
    chosen = crank < float(PEER_TOPK)
    z = jnp.sum(jnp.where(chosen, jnp.exp(cand - top), 0.0), axis=0, keepdims=True)
    ci = jnp.zeros(rank1.shape, F32)
    row = 0
    for a in range(PEER_TOPK):
        nb = PEER_TOPK // (a + 1)
        cnt = jnp.sum(jnp.where(chosen[row:row + nb], 1.0, 0.0), axis=0, keepdims=True)
        ci = jnp.where(rank1 == float(a), cnt, ci)
        row += nb
    r2_ref[0] = rank2.astype(BF16)
    bw_ref[0] = (jnp.exp(s2 - v2[0]) / z).astype(BF16)
    ci_ref[0] = ci
    ai_ref[0] = jnp.exp(s1 - v1[0])


def _route_body(qp_ref, kz_ref, r2_ref, bw_ref, ci_ref, ai_ref):
    nk = PEER_NKEYS
    st = lax.dot_general(kz_ref[0], qp_ref[...], _NT, preferred_element_type=F32)
    s1, s2 = st[:nk], st[nk:]
    outs = (r2_ref, bw_ref, ci_ref, ai_ref)
    v1, rank1, tied1 = _extract_top_untied(s1, PEER_TOPK)
    v2, rank2, tied2 = _extract_top_untied(s2, PEER_TOPK)
    _route_outputs(s1, s2, v1, rank1, v2, rank2, *outs)

    @pl.when(jnp.maximum(tied1, tied2) > 0.0)
    def _():
        _route_outputs(s1, s2, *_extract_top(s1, PEER_TOPK), *_extract_top(s2, PEER_TOPK), *outs)


def _peer_route(qp, kz, tt):
    n, w = qp.shape
    hw = w // PEER_HEADS
    spec = pl.BlockSpec((1, PEER_NKEYS, tt), lambda t, h: (h, 0, t))
    shp = lambda dt: jax.ShapeDtypeStruct((PEER_HEADS, PEER_NKEYS, n), dt)
    return pl.pallas_call(
        _route_body,
        grid=(n // tt, PEER_HEADS),
        in_specs=[pl.BlockSpec((tt, hw), lambda t, h: (t, h)),
                  pl.BlockSpec((1,) + kz.shape[1:], lambda t, h: (h, 0, 0))],
        out_specs=[spec, spec, spec, spec],
        out_shape=[shp(BF16), shp(BF16), shp(F32), shp(F32)],
        compiler_params=_params("parallel", "parallel"),
        name="peer_route",
    )(qp, kz)


_EG = 512
_OB = 512


def _row_tile(ref, h, i):
    row = ref[h, pl.ds(i, 1), :]
    return jnp.broadcast_to(row, (_BT, row.shape[1])).astype(BF16)


def _experts_body(xn_ref, u_ref, vt_ref, r2_ref, bw_ref, ci_ref, ai_ref, h_ref, fw_ref, y_ref, acc_sc, *, ec):
    c = pl.program_id(1)
    nk = PEER_NKEYS

    @pl.when(c == 0)
    def _():
        acc_sc[...] = jnp.zeros(acc_sc.shape, F32)

    xn = xn_ref[...]
    tokens = xn.shape[0]
    ngroups = ec // _EG

    def hidden(g):
        return lax.dot_general(u_ref[g * _EG:(g + 1) * _EG, :], xn, _NT, preferred_element_type=F32)

    nxt = hidden(0)
    for g in range(ngroups):
        hid = nxt
        if g + 1 < ngroups:
            nxt = hidden(g + 1)
        acts = []
        for gi in range(_EG // nk):
            i = (c * ec + g * _EG) // nk + gi
            w = None
            for h in range(PEER_HEADS):
                ci = _row_tile(ci_ref, h, i)[None]
                ai = _row_tile(ai_ref, h, i)[None]
                r2 = r2_ref[h].reshape(nk // _BT, _BT, tokens)
                bw = bw_ref[h].reshape(nk // _BT, _BT, tokens)
                term = jnp.where(r2 < ci, bw, jnp.zeros((), BF16)) * ai
                w = term if w is None else w + term
            hg = hid[gi * nk:(gi + 1) * nk]
            gelu = 0.5 * hg * (1.0 + lax.erf(hg * (2.0 ** -0.5)))
            acts.append((gelu * w.reshape(nk, tokens).astype(F32)).astype(BF16))
        act = jnp.concatenate(acts, axis=0)
        for lo in range(0, acc_sc.shape[0], _OB):
            acc_sc[lo:lo + _OB, :] += jnp.dot(vt_ref[lo:lo + _OB, g * _EG:(g + 1) * _EG], act,
                                              preferred_element_type=F32)

    @pl.when(c == pl.num_programs(1) - 1)
    def _():
        hfin = h_ref[...] + acc_sc[...].T
        y_ref[...] = _rms(hfin, fw_ref[...])


def _peer_experts(xn, u_bf16, vt_bf16, r2, bw, ci, ai, h1, final_w, tt, ec):
    n, d = xn.shape
    ne = u_bf16.shape[0]
    gate = pl.BlockSpec((PEER_HEADS, PEER_NKEYS, tt), lambda t, c: (0, 0, t))
    return pl.pallas_call(
        functools.partial(_experts_body, ec=ec),
        grid=(n // tt, ne // ec),
        in_specs=[pl.BlockSpec((tt, d), lambda t, c: (t, 0)),
                  pl.BlockSpec((ec, d), lambda t, c: (c, 0)),
                  pl.BlockSpec((d, ec), lambda t, c: (0, c)),
                  gate, gate, gate, gate,
                  pl.BlockSpec((tt, d), lambda t, c: (t, 0)),
                  pl.BlockSpec((1, d), lambda t, c: (0, 0))],
        out_specs=pl.BlockSpec((tt, d), lambda t, c: (t, 0)),
        out_shape=jax.ShapeDtypeStruct((n, d), F32),
        scratch_shapes=[pltpu.VMEM((d, tt), F32)],
        compiler_params=_params("parallel", "arbitrary"),
        name="peer_experts",
    )(xn, u_bf16, vt_bf16, r2, bw, ci, ai, h1, final_w.reshape(1, d))


def _peer_key_blocks(keys):
    z = jnp.zeros_like(keys[:, 0])
    top = jnp.concatenate([keys[:, 0], z], axis=-1)
    bot = jnp.concatenate([z, keys[:, 1]], axis=-1)
    return jnp.concatenate([top, bot], axis=1).astype(BF16)


def kernel(x, norm1_w, w_in, b_gate, diff_lq1, diff_lk1, diff_lq2, diff_lk2, diff_subln_w, chunk_rel_bias,
           w_branch_diff, w_branch_chunk, w_out, norm2_w, peer_wq, peer_keys, peer_u, peer_v, final_norm_w):
    batch, seq, d = x.shape
    depth = norm1_w.shape[0]
    n = batch * seq
    h = x.reshape(n, d)
    rel_idx = _rel_row_index()
    for l in range(depth):
        lam_init = 0.8 - 0.6 * math.exp(-0.3 * l)
        qd, kd, vdt, qc, kc, vct, gates = _in_proj(h, norm1_w[l], w_in[l].astype(BF16), tm=min(512, seq))
        lam_vecs = jnp.stack([diff_lq1[l], diff_lk1[l], diff_lq2[l], diff_lk2[l]])
        od = _diff_attn(qd, kd, vdt, lam_vecs, diff_subln_w[l], batch, seq, lam_init)
        oc = _chunk_attn(qc, kc, vct, chunk_rel_bias[l][:, rel_idx], batch, seq)
        h1, xn2, qp = _merge(od, oc, gates, h, b_gate[l], w_branch_diff[l].astype(BF16),
                             w_branch_chunk[l].astype(BF16), w_out[l].astype(BF16), norm2_w[l],
                             peer_wq[l].astype(BF16), tm=min(512, n))
        tt = min(512, n)
        r2, bw, ci, ai = _peer_route(qp, _peer_key_blocks(peer_keys[l]), tt)
        last = l == depth - 1
        fw = final_norm_w if last else jnp.ones((d,), F32)
        h = _peer_experts(xn2, peer_u[l].astype(BF16), peer_v[l].T.astype(BF16), r2, bw, ci, ai, h1, fw,
                          tt=tt, ec=2048)
        assert last, "the fused final RMSNorm assumes a single layer"
    return h.reshape(batch, seq, d)
```

```python
import functools
import math

import jax
import jax.numpy as jnp
import numpy as np
from jax import lax
from jax.experimental import pallas as pl
from jax.experimental.pallas import tpu as pltpu

F32 = jnp.float32
BF16 = jnp.bfloat16

EPS = 1e-6
CHUNK = 64
DIFF_HEADS = 8
DIFF_DH = 64
CH_HEADS = 8
CH_DH = 64
CH_LEFT = 8
REL_CLIP = 128
PEER_HEADS = 8
PEER_NKEYS = 128
PEER_TOPK = 16
NEG = -1e30
_BT = 16

VMEM_LIMIT = 56 * 1024 * 1024

_NT = (((1,), (1,)), ((), ()))


def _params(*sem):
    return pltpu.CompilerParams(dimension_semantics=sem, vmem_limit_bytes=VMEM_LIMIT)


def _rms(x, w):
    return x * lax.rsqrt(jnp.mean(x * x, axis=-1, keepdims=True) + EPS) * w


_QKV_WIDTHS = (1024, 1024, 1024, 512, 512, 512)
LOG2E = math.log2(math.e)
_Q_SCALE = (DIFF_DH ** -0.5 * LOG2E, None, None, CH_DH ** -0.5 * LOG2E, None, None)
_TRANSPOSED = (2, 5)


def _inproj_body(x_ref, nw_ref, w_ref, wvt_ref, qd_ref, kd_ref, vdt_ref, qc_ref, kc_ref, vct_ref, g_ref):
    xn = _rms(x_ref[...], nw_ref[...]).astype(BF16)
    col = 0
    outs = (qd_ref, kd_ref, None, qc_ref, kc_ref, None)
    for ref, width, scale in zip(outs, _QKV_WIDTHS, _Q_SCALE):
        if ref is not None:
            acc = jnp.dot(xn, w_ref[:, col:col + width], preferred_element_type=F32)
            if scale is not None:
                acc = acc * scale
            ref[...] = acc.astype(BF16)
        col += width
    g_ref[...] = jnp.dot(xn, w_ref[:, col:], preferred_element_type=F32)
    vt = lax.dot_general(wvt_ref[...], xn, _NT, preferred_element_type=F32).astype(BF16)
    nd = vdt_ref.shape[1]
    vdt_ref[0] = vt[:nd]
    vct_ref[0] = vt[nd:]


def _in_proj(x2, norm_w, w_in_bf16, tm):
    n, d = x2.shape
    cols = w_in_bf16.shape[1]
    starts = np.cumsum((0,) + _QKV_WIDTHS)
    wvt = jnp.concatenate([w_in_bf16[:, starts[k]:starts[k + 1]].T for k in _TRANSPOSED], axis=0)
    row = lambda width: pl.BlockSpec((tm, width), lambda i: (i, 0))
    const = lambda shape: pl.BlockSpec(shape, lambda i: (0, 0))
    out_specs = [row(w) for w in _QKV_WIDTHS] + [row(2 * d)]
    out_shape = [jax.ShapeDtypeStruct((n, w), BF16) for w in _QKV_WIDTHS] + [jax.ShapeDtypeStruct((n, 2 * d), F32)]
    for k in _TRANSPOSED:
        out_specs[k] = pl.BlockSpec((1, _QKV_WIDTHS[k], tm), lambda i: (i, 0, 0))
        out_shape[k] = jax.ShapeDtypeStruct((n // tm, _QKV_WIDTHS[k], tm), BF16)
    return pl.pallas_call(
        _inproj_body,
        grid=(n // tm,),
        in_specs=[row(d), const((1, d)), const((d, cols)), const(wvt.shape)],
        out_specs=out_specs,
        out_shape=out_shape,
        compiler_params=_params("parallel"),
        name="in_proj",
    )(x2, norm_w.reshape(1, d), w_in_bf16, wvt)


_QS = 256


def _diff_attn_body(q_ref, k_ref, vt_ref, lam_ref, subw_ref, o_ref, off_sc, diag_sc, sa_sc, sb_sc, ma_sc,
                    mb_sc, acc_sc, *, t, lam_init):
    h = pl.program_id(1)
    i = pl.program_id(2)
    slope = lax.bitcast_convert_type(jnp.full((1, 1), 126 - h, jnp.int32) << 23, F32) * LOG2E

    @pl.when(i == 0)
    def _():
        kpos = lax.broadcasted_iota(jnp.int32, (t, t), 0)
        qpos = lax.broadcasted_iota(jnp.int32, (t, t), 1)
        dist = (qpos - kpos).astype(F32)
        off_sc[...] = -slope * dist
        diag_sc[...] = jnp.where((kpos // CHUNK) <= (qpos // CHUNK), -slope * jnp.abs(dist), NEG)

    q = q_ref[...]
    lane = lax.broadcasted_iota(jnp.int32, q.shape, 1)
    zero = jnp.zeros_like(q)
    qz = jnp.concatenate([jnp.where(lane < DIFF_DH, q, zero), jnp.where(lane >= DIFF_DH, q, zero)], axis=0)

    buf_a, buf_b = (sa_sc, ma_sc), (sb_sc, mb_sc)

    def scores(j, bias_ref, bufs):
        s_ref, mx_ref = bufs
        kb = k_ref[pl.ds(pl.multiple_of(j * t, t), t), :]
        s = lax.dot_general(kb, qz, _NT, preferred_element_type=F32)
        bias = bias_ref[...]
        s = jnp.concatenate([s[:, :t] + bias, s[:, t:] + bias], axis=1)
        s_ref[...] = s
        mx_ref[...] = jnp.max(s, axis=0, keepdims=True)

    def tile(j, bufs, shift, carry):
        s_ref, mx_ref = bufs
        vtb = vt_ref[j]
        new = []
        for n in range(2 * t // _QS):
            lo = n * _QS
            m_old, l_old = (c[:, lo:lo + _QS] for c in carry)
            m_new = jnp.maximum(m_old, mx_ref[:, lo:lo + _QS] + shift)
            alpha = jnp.exp2(m_old - m_new)
            p = jnp.exp2(s_ref[:, lo:lo + _QS] - (m_new - shift))
            l_new = alpha * l_old + jnp.sum(p, axis=0, keepdims=True)
            acc_sc[:, lo:lo + _QS] = (alpha * acc_sc[:, lo:lo + _QS]
                                      + jnp.dot(vtb, p.astype(BF16), preferred_element_type=F32))
            new.append((m_new, l_new))
        return tuple(jnp.concatenate(parts, axis=1) for parts in zip(*new))

    def off_tile(o, bufs, carry):
        return tile(o, bufs, -slope * ((i - o) * t).astype(F32), carry)

    def pair(p, carry):
        o = 2 * p
        scores(o + 1, off_sc, buf_a)
        carry = off_tile(o, buf_b, carry)
        scores(jnp.minimum(o + 2, i - 1), off_sc, buf_b)
        return off_tile(o + 1, buf_a, carry)

    init = (jnp.full((1, 2 * t), NEG, F32), jnp.zeros((1, 2 * t), F32))
    acc_sc[...] = jnp.zeros(acc_sc.shape, F32)
    scores(i, diag_sc, buf_a)
    scores(0, off_sc, buf_b)
    carry = tile(i, buf_a, jnp.zeros((1, 1), F32), init)
    carry = lax.fori_loop(0, i // 2, pair, carry)
    _, l = lax.cond(i % 2 == 1, functools.partial(off_tile, i - 1, buf_b), lambda c: c, carry)
    acc = acc_sc[...]

    o = acc / l
    lv = lam_ref[...]
    lam = (jnp.exp(jnp.sum(lv[0:1] * lv[1:2], axis=-1, keepdims=True))
           - jnp.exp(jnp.sum(lv[2:3] * lv[3:4], axis=-1, keepdims=True)) + lam_init)
    o = o[:, :t] - lam * o[:, t:]
    y = o * lax.rsqrt(jnp.mean(o * o, axis=0, keepdims=True) + EPS) * (subw_ref[...] * (1.0 - lam_init))
    o_ref[...] = y.T.astype(BF16)


def _diff_attn(qd, kd, vdt, lam_vecs, subln_w, batch, seq, lam_init):
    n = qd.shape[0]
    t = vdt.shape[2]
    nq = seq // t
    hw = 2 * DIFF_DH
    return pl.pallas_call(
        functools.partial(_diff_attn_body, t=t, lam_init=lam_init),
        grid=(batch, DIFF_HEADS, nq),
        in_specs=[
            pl.BlockSpec((t, hw), lambda b, h, i: (b * nq + i, h)),
            pl.BlockSpec((seq, hw), lambda b, h, i: (b, h)),
            pl.BlockSpec((nq, hw, t), lambda b, h, i: (b, h, 0)),
            pl.BlockSpec((4, DIFF_DH), lambda b, h, i: (0, 0)),
            pl.BlockSpec((hw, 1), lambda b, h, i: (0, 0)),
        ],
        out_specs=pl.BlockSpec((t, hw), lambda b, h, i: (b * nq + i, h)),
        out_shape=jax.ShapeDtypeStruct((n, DIFF_HEADS * hw), BF16),
        scratch_shapes=[pltpu.VMEM((t, t), F32), pltpu.VMEM((t, t), F32),
                        pltpu.VMEM((t, 2 * t), F32), pltpu.VMEM((t, 2 * t), F32),
                        pltpu.VMEM((1, 2 * t), F32), pltpu.VMEM((1, 2 * t), F32), pltpu.VMEM((hw, 2 * t), F32)],
        compiler_params=_params("parallel", "parallel", "arbitrary"),
        name="diff_attn",
    )(qd, kd, vdt, lam_vecs, subln_w.reshape(hw, 1))


_CT = 256
_CWIN = 3 * _CT
_FW = 1024
_HP = 2 * CH_DH


def _rel_row_index():
    u = (-np.arange(_FW)) % _FW
    u = np.where(u < _CWIN, u, u - _FW)
    rel = CH_LEFT * CHUNK - u
    return np.clip(rel, -REL_CLIP, REL_CLIP) + REL_CLIP


def _chunk_attn_body(q_ref, k0_ref, k1_ref, k2_ref, v0_ref, v1_ref, v2_ref, f_ref, o_ref, bias_sc):
    i = pl.program_id(1)

    @pl.when((pl.program_id(0) == 0) & (i == 0))
    def _():
        c = lax.broadcasted_iota(jnp.int32, (_CWIN, _CT), 0) // CHUNK
        r = lax.broadcasted_iota(jnp.int32, (_CWIN, _CT), 1) // CHUNK
        allowed = (c >= r) & (c <= r + CH_LEFT)
        for h in range(CH_HEADS):
            row = jnp.broadcast_to(f_ref[h:h + 1, :], (_CWIN, _FW))
            toeplitz = pltpu.roll(row, 0, 1, stride=1, stride_axis=0)
            bias_sc[h] = jnp.where(allowed, toeplitz[:, :_CT] * LOG2E, NEG)

    def attend(clamped):
        q = q_ref[...]
        lane = lax.broadcasted_iota(jnp.int32, (_CT, _HP), 1)
        zero = jnp.zeros((_CT, _HP), BF16)
        outs = []
        for hp in range(CH_HEADS // 2):
            sl = slice(hp * _HP, (hp + 1) * _HP)
            qp = q[:, sl]
            qz = jnp.concatenate([jnp.where(lane < CH_DH, qp, zero), jnp.where(lane >= CH_DH, qp, zero)], axis=0)
            kp = jnp.concatenate([k0_ref[:, sl], k1_ref[:, sl], k2_ref[:, sl]], axis=0)
            s = lax.dot_general(kp, qz, _NT, preferred_element_type=F32)
            s = jnp.concatenate([s[:, :_CT] + bias_sc[2 * hp], s[:, _CT:] + bias_sc[2 * hp + 1]], axis=1)
            if clamped:
                pen = [jnp.where(i - 2 + d >= 0, 0.0, NEG).astype(F32) for d in range(2)]
                s = jnp.concatenate([s[:_CT] + pen[0], s[_CT:2 * _CT] + pen[1], s[2 * _CT:]], axis=0)
            m = jnp.max(s, axis=0, keepdims=True)
            p = jnp.exp2(s - m)
            l = jnp.sum(p, axis=0, keepdims=True)
            vt = jnp.concatenate([v0_ref[0, sl, :], v1_ref[0, sl, :], v2_ref[0, sl, :]], axis=1)
            o = jnp.dot(vt, p.astype(BF16), preferred_element_type=F32) / l
            outs += [o[:CH_DH, :_CT], o[CH_DH:, _CT:]]
        o_ref[...] = jnp.concatenate(outs, axis=0).T.astype(BF16)

    @pl.when(i < 2)
    def _():
        attend(True)

    @pl.when(i >= 2)
    def _():
        attend(False)


def _chunk_attn(qc, kc, vct, rel_rows, batch, seq):
    n, w = qc.shape
    nq = seq // _CT
    per_tile = vct.shape[2] // _CT
    block = lambda b, i, d: b * nq + jnp.maximum(i - 2 + d, 0)
    kwin = lambda d: pl.BlockSpec((_CT, w), lambda b, i: (block(b, i, d), 0))
    vwin = lambda d: pl.BlockSpec((1, w, _CT), lambda b, i: (block(b, i, d) // per_tile, 0, block(b, i, d) % per_tile))
    return pl.pallas_call(
        _chunk_attn_body,
        grid=(batch, nq),
        in_specs=[pl.BlockSpec((_CT, w), lambda b, i: (b * nq + i, 0)),
                  kwin(0), kwin(1), kwin(2), vwin(0), vwin(1), vwin(2),
                  pl.BlockSpec((CH_HEADS, _FW), lambda b, i: (0, 0))],
        out_specs=pl.BlockSpec((_CT, w), lambda b, i: (b * nq + i, 0)),
        out_shape=jax.ShapeDtypeStruct((n, w), BF16),
        scratch_shapes=[pltpu.VMEM((CH_HEADS, _CWIN, _CT), F32)],
        compiler_params=_params("arbitrary", "arbitrary"),
        name="chunk_attn",
    )(qc, kc, kc, kc, vct, vct, vct, rel_rows)


def _merge_body(od_ref, oc_ref, g_ref, x_ref, bg_ref, wbd_ref, wbc_ref, wo_ref, n2_ref, wq_ref,
                h_ref, xn_ref, qp_ref):
    d = x_ref.shape[1]
    g = g_ref[...] + bg_ref[...]
    pd = jnp.dot(od_ref[...], wbd_ref[...], preferred_element_type=F32)
    pc = jnp.dot(oc_ref[...], wbc_ref[...], preferred_element_type=F32)
    merged = jax.nn.sigmoid(g[:, :d]) * pd + jax.nn.sigmoid(g[:, d:]) * pc
    h1 = x_ref[...] + jnp.dot(merged.astype(BF16), wo_ref[...], preferred_element_type=F32)
    h_ref[...] = h1
    xn = _rms(h1, n2_ref[...]).astype(BF16)
    xn_ref[...] = xn
    qp_ref[...] = jnp.dot(xn, wq_ref[...], preferred_element_type=F32).astype(BF16)


def _merge(od, oc, gates, x2, b_gate, wbd, wbc, wo, norm2_w, wq, tm):
    n, d = x2.shape
    row = lambda width: pl.BlockSpec((tm, width), lambda i: (i, 0))
    const = lambda shape: pl.BlockSpec(shape, lambda i: (0, 0))
    return pl.pallas_call(
        _merge_body,
        grid=(n // tm,),
        in_specs=[row(od.shape[1]), row(oc.shape[1]), row(2 * d), row(d), const((1, 2 * d)),
                  const(wbd.shape), const(wbc.shape), const(wo.shape), const((1, d)), const(wq.shape)],
        out_specs=[row(d), row(d), row(wq.shape[1])],
        out_shape=[jax.ShapeDtypeStruct((n, d), F32), jax.ShapeDtypeStruct((n, d), BF16),
                   jax.ShapeDtypeStruct((n, wq.shape[1]), BF16)],
        compiler_params=_params("parallel"),
        name="merge",
    )(od, oc, gates, x2, b_gate.reshape(1, 2 * d), wbd, wbc, wo, norm2_w.reshape(1, d), wq)


def _pair_list():
    return [(a, b) for a in range(PEER_TOPK) for b in range(PEER_TOPK) if (a + 1) * (b + 1) <= PEER_TOPK]


def _extract_top(s, steps):
    rows = s.shape[0]
    idx = lax.broadcasted_iota(jnp.int32, s.shape, 0).astype(F32)
    rank = jnp.full(s.shape, float(steps), F32)
    vals = []
    for step in range(steps):
        m = jnp.max(s, axis=0, keepdims=True)
        first = jnp.min(jnp.where(s == m, idx, float(rows)), axis=0, keepdims=True)
        sel = idx == first
        rank = jnp.where(sel, float(step), rank)
        s = jnp.where(sel, -jnp.inf, s)
        vals.append(m)
    return vals, rank


def _extract_top_untied(s, steps):
    rank = jnp.full(s.shape, float(steps), F32)
    vals = []
    for step in range(steps):
        m = jnp.max(s, axis=0, keepdims=True)
        sel = s == m
        rank = jnp.where(sel, float(step), rank)
        s = jnp.where(sel, -jnp.inf, s)
        vals.append(m)
    removed = jnp.sum(jnp.where(rank < float(steps), 1.0, 0.0), axis=0, keepdims=True)
    return vals, rank, jnp.max(jnp.abs(removed - float(steps)))


def _route_outputs(s1, s2, v1, rank1, v2, rank2, r2_ref, bw_ref, ci_ref, ai_ref):
    pairs = _pair_list()
    cand = jnp.concatenate([v1[a] + v2[b] for a, b in pairs], axis=0)
    top = cand[0:1]
    _, crank = _extract_top(cand, PEER_TOPK)
    chosen = crank < float(PEER_TOPK)
    z = jnp.sum(jnp.where(chosen, jnp.exp(cand - top), 0.0), axis=0, keepdims=True)
    ci = jnp.zeros(rank1.shape, F32)
    row = 0
    for a in range(PEER_TOPK):
        nb = PEER_TOPK // (a + 1)
        cnt = jnp.sum(jnp.where(chosen[row:row + nb], 1.0, 0.0), axis=0, keepdims=True)
        ci = jnp.where(rank1 == float(a), cnt, ci)
        row += nb
    r2_ref[0] = rank2.astype(BF16)
    bw_ref[0] = (jnp.exp(s2 - v2[0]) / z).astype(BF16)
    ci_ref[0] = ci
    ai_ref[0] = jnp.exp(s1 - v1[0])


def _route_body(qp_ref, kz_ref, r2_ref, bw_ref, ci_ref, ai_ref):
    nk = PEER_NKEYS
    st = lax.dot_general(kz_ref[0], qp_ref[...], _NT, preferred_element_type=F32)
    s1, s2 = st[:nk], st[nk:]
    outs = (r2_ref, bw_ref, ci_ref, ai_ref)
    v1, rank1, tied1 = _extract_top_untied(s1, PEER_TOPK)
    v2, rank2, tied2 = _extract_top_untied(s2, PEER_TOPK)
    _route_outputs(s1, s2, v1, rank1, v2, rank2, *outs)

    @pl.when(jnp.maximum(tied1, tied2) > 0.0)
    def _():
        _route_outputs(s1, s2, *_extract_top(s1, PEER_TOPK), *_extract_top(s2, PEER_TOPK), *outs)


def _peer_route(qp, kz, tt):
    n, w = qp.shape
    hw = w // PEER_HEADS
    spec = pl.BlockSpec((1, PEER_NKEYS, tt), lambda t, h: (h, 0, t))
    shp = lambda dt: jax.ShapeDtypeStruct((PEER_HEADS, PEER_NKEYS, n), dt)
    return pl.pallas_call(
        _route_body,
        grid=(n // tt, PEER_HEADS),
        in_specs=[pl.BlockSpec((tt, hw), lambda t, h: (t, h)),
                  pl.BlockSpec((1,) + kz.shape[1:], lambda t, h: (h, 0, 0))],
        out_specs=[spec, spec, spec, spec],
        out_shape=[shp(BF16), shp(BF16), shp(F32), shp(F32)],
        compiler_params=_params("parallel", "parallel"),
        name="peer_route",
    )(qp, kz)


_EG = 1024
_OB = 1024


def _row_tile(ref, h, i):
    row = ref[h, pl.ds(i, 1), :]
    return jnp.broadcast_to(row, (_BT, row.shape[1])).astype(BF16)


def _experts_body(xn_ref, u_ref, vt_ref, r2_ref, bw_ref, ci_ref, ai_ref, h_ref, fw_ref, y_ref, acc_sc, *, ec):
    c = pl.program_id(1)
    nk = PEER_NKEYS

    @pl.when(c == 0)
    def _():
        acc_sc[...] = jnp.zeros(acc_sc.shape, F32)

    xn = xn_ref[...]
    tokens = xn.shape[0]
    ngroups = ec // _EG

    def hidden(g):
        return lax.dot_general(u_ref[g * _EG:(g + 1) * _EG, :], xn, _NT, preferred_element_type=F32)

    nxt = hidden(0)
    for g in range(ngroups):
        hid = nxt
        if g + 1 < ngroups:
            nxt = hidden(g + 1)
        acts = []
        for gi in range(_EG // nk):
            i = (c * ec + g * _EG) // nk + gi
            w = None
            for h in range(PEER_HEADS):
                ci = _row_tile(ci_ref, h, i)[None]
                ai = _row_tile(ai_ref, h, i)[None]
                r2 = r2_ref[h].reshape(nk // _BT, _BT, tokens)
                bw = bw_ref[h].reshape(nk // _BT, _BT, tokens)
                term = jnp.where(r2 < ci, bw, jnp.zeros((), BF16)) * ai
                w = term if w is None else w + term
            hg = hid[gi * nk:(gi + 1) * nk]
            gelu = 0.5 * hg * (1.0 + lax.erf(hg * (2.0 ** -0.5)))
            acts.append((gelu * w.reshape(nk, tokens).astype(F32)).astype(BF16))
        act = jnp.concatenate(acts, axis=0)
        for lo in range(0, acc_sc.shape[0], _OB):
            acc_sc[lo:lo + _OB, :] += jnp.dot(vt_ref[lo:lo + _OB, g * _EG:(g + 1) * _EG], act,
                                              preferred_element_type=F32)

    @pl.when(c == pl.num_programs(1) - 1)
    def _():
        hfin = h_ref[...] + acc_sc[...].T
        y_ref[...] = _rms(hfin, fw_ref[...])


def _peer_experts(xn, u_bf16, vt_bf16, r2, bw, ci, ai, h1, final_w, tt, ec):
    n, d = xn.shape
    ne = u_bf16.shape[0]
    gate = pl.BlockSpec((PEER_HEADS, PEER_NKEYS, tt), lambda t, c: (0, 0, t))
    return pl.pallas_call(
        functools.partial(_experts_body, ec=ec),
        grid=(n // tt, ne // ec),
        in_specs=[pl.BlockSpec((tt, d), lambda t, c: (t, 0)),
                  pl.BlockSpec((ec, d), lambda t, c: (c, 0)),
                  pl.BlockSpec((d, ec), lambda t, c: (0, c)),
                  gate, gate, gate, gate,
                  pl.BlockSpec((tt, d), lambda t, c: (t, 0)),
                  pl.BlockSpec((1, d), lambda t, c: (0, 0))],
        out_specs=pl.BlockSpec((tt, d), lambda t, c: (t, 0)),
        out_shape=jax.ShapeDtypeStruct((n, d), F32),
        scratch_shapes=[pltpu.VMEM((d, tt), F32)],
        compiler_params=_params("parallel", "arbitrary"),
        name="peer_experts",
    )(xn, u_bf16, vt_bf16, r2, bw, ci, ai, h1, final_w.reshape(1, d))


def _peer_key_blocks(keys):
    z = jnp.zeros_like(keys[:, 0])
    top = jnp.concatenate([keys[:, 0], z], axis=-1)
    bot = jnp.concatenate([z, keys[:, 1]], axis=-1)
    return jnp.concatenate([top, bot], axis=1).astype(BF16)


def kernel(x, norm1_w, w_in, b_gate, diff_lq1, diff_lk1, diff_lq2, diff_lk2, diff_subln_w, chunk_rel_bias,
           w_branch_diff, w_branch_chunk, w_out, norm2_w, peer_wq, peer_keys, peer_u, peer_v, final_norm_w):
    batch, seq, d = x.shape
    depth = norm1_w.shape[0]
    n = batch * seq
    h = x.reshape(n, d)
    rel_idx = _rel_row_index()
    for l in range(depth):
        lam_init = 0.8 - 0.6 * math.exp(-0.3 * l)
        qd, kd, vdt, qc, kc, vct, gates = _in_proj(h, norm1_w[l], w_in[l].astype(BF16), tm=min(512, seq))
        lam_vecs = jnp.stack([diff_lq1[l], diff_lk1[l], diff_lq2[l], diff_lk2[l]])
        od = _diff_attn(qd, kd, vdt, lam_vecs, diff_subln_w[l], batch, seq, lam_init)
        oc = _chunk_attn(qc, kc, vct, chunk_rel_bias[l][:, rel_idx], batch, seq)
        h1, xn2, qp = _merge(od, oc, gates, h, b_gate[l], w_branch_diff[l].astype(BF16),
                             w_branch_chunk[l].astype(BF16), w_out[l].astype(BF16), norm2_w[l],
                             peer_wq[l].astype(BF16), tm=min(512, n))
        tt = min(512, n)
        r2, bw, ci, ai = _peer_route(qp, _peer_key_blocks(peer_keys[l]), tt)
        last = l == depth - 1
        fw = final_norm_w if last else jnp.ones((d,), F32)
        h = _peer_experts(xn2, peer_u[l].astype(BF16), peer_v[l].T.astype(BF16), r2, bw, ci, ai, h1, fw,
                          tt=tt, ec=2048)
        assert last, "the fused final RMSNorm assumes a single layer"
    return h.reshape(batch, seq, d)
```

```python
import functools
import math

import jax
import jax.numpy as jnp
import numpy as np
from jax import lax
from jax.experimental import pallas as pl
from jax.experimental.pallas import tpu as pltpu

F32 = jnp.float32
BF16 = jnp.bfloat16

EPS = 1e-6
CHUNK = 64
DIFF_HEADS = 8
DIFF_DH = 64
CH_HEADS = 8
CH_DH = 64
CH_LEFT = 8
REL_CLIP = 128
PEER_HEADS = 8
PEER_NKEYS = 128
PEER_TOPK = 16
NEG = -1e30
_BT = 16

VMEM_LIMIT = 56 * 1024 * 1024

_NT = (((1,), (1,)), ((), ()))


def _params(*sem):
    return pltpu.CompilerParams(dimension_semantics=sem, vmem_limit_bytes=VMEM_LIMIT)


def _rms(x, w):
    return x * lax.rsqrt(jnp.mean(x * x, axis=-1, keepdims=True) + EPS) * w


_QKV_WIDTHS = (1024, 1024, 1024, 512, 512, 512)
LOG2E = math.log2(math.e)
_Q_SCALE = (DIFF_DH ** -0.5 * LOG2E, None, None, CH_DH ** -0.5 * LOG2E, None, None)
_TRANSPOSED = (2, 5)


def _inproj_body(x_ref, nw_ref, w_ref, wvt_ref, qd_ref, kd_ref, vdt_ref, qc_ref, kc_ref, vct_ref, g_ref):
    xn = _rms(x_ref[...], nw_ref[...]).astype(BF16)
    col = 0
    outs = (qd_ref, kd_ref, None, qc_ref, kc_ref, None)
    for ref, width, scale in zip(outs, _QKV_WIDTHS, _Q_SCALE):
        if ref is not None:
            acc = jnp.dot(xn, w_ref[:, col:col + width], preferred_element_type=F32)
            if scale is not None:
                acc = acc * scale
            ref[...] = acc.astype(BF16)
        col += width
    g_ref[...] = jnp.dot(xn, w_ref[:, col:], preferred_element_type=F32)
    vt = lax.dot_general(wvt_ref[...], xn, _NT, preferred_element_type=F32).astype(BF16)
    nd = vdt_ref.shape[1]
    vdt_ref[0] = vt[:nd]
    vct_ref[0] = vt[nd:]


def _in_proj(x2, norm_w, w_in_bf16, tm):
    n, d = x2.shape
    cols = w_in_bf16.shape[1]
    starts = np.cumsum((0,) + _QKV_WIDTHS)
    wvt = jnp.concatenate([w_in_bf16[:, starts[k]:starts[k + 1]].T for k in _TRANSPOSED], axis=0)
    row = lambda width: pl.BlockSpec((tm, width), lambda i: (i, 0))
    const = lambda shape: pl.BlockSpec(shape, lambda i: (0, 0))
    out_specs = [row(w) for w in _QKV_WIDTHS] + [row(2 * d)]
    out_shape = [jax.ShapeDtypeStruct((n, w), BF16) for w in _QKV_WIDTHS] + [jax.ShapeDtypeStruct((n, 2 * d), F32)]
    for k in _TRANSPOSED:
        out_specs[k] = pl.BlockSpec((1, _QKV_WIDTHS[k], tm), lambda i: (i, 0, 0))
        out_shape[k] = jax.ShapeDtypeStruct((n // tm, _QKV_WIDTHS[k], tm), BF16)
    return pl.pallas_call(
        _inproj_body,
        grid=(n // tm,),
        in_specs=[row(d), const((1, d)), const((d, cols)), const(wvt.shape)],
        out_specs=out_specs,
        out_shape=out_shape,
        compiler_params=_params("parallel"),
        name="in_proj",
    )(x2, norm_w.reshape(1, d), w_in_bf16, wvt)


_QS = 256
_HPS = 2


def _diff_attn_body(q_ref, k_ref, vt_ref, lam_ref, subw_ref, o_ref, off_sc, diag_sc, sa_sc, sb_sc, ma_sc,
                    mb_sc, acc_sc, *, t, lam_init):
    hp = pl.program_id(1)
    i = pl.program_id(2)
    hw = 2 * DIFF_DH
    heads = range(_HPS)
    slopes = [lax.bitcast_convert_type(jnp.full((1, 1), 126 - (hp * _HPS + e), jnp.int32) << 23, F32) * LOG2E
              for e in heads]

    @pl.when(i == 0)
    def _():
        kpos = lax.broadcasted_iota(jnp.int32, (t, t), 0)
        qpos = lax.broadcasted_iota(jnp.int32, (t, t), 1)
        dist = (qpos - kpos).astype(F32)
        for e in heads:
            off_sc[e] = -slopes[e] * dist
            diag_sc[e] = jnp.where((kpos // CHUNK) <= (qpos // CHUNK), -slopes[e] * jnp.abs(dist), NEG)

    lane = lax.broadcasted_iota(jnp.int32, (t, hw), 1)
    zero = jnp.zeros((t, hw), BF16)
    qz = []
    for e in heads:
        q = q_ref[:, e * hw:(e + 1) * hw]
        qz.append(jnp.concatenate([jnp.where(lane < DIFF_DH, q, zero), jnp.where(lane >= DIFF_DH, q, zero)], axis=0))

    buf_a, buf_b = (sa_sc, ma_sc), (sb_sc, mb_sc)

    def scores(j, bias_sc, bufs):
        s_ref, mx_ref = bufs
        for e in heads:
            kb = k_ref[pl.ds(pl.multiple_of(j * t, t), t), e * hw:(e + 1) * hw]
            s = lax.dot_general(kb, qz[e], _NT, preferred_element_type=F32)
            bias = bias_sc[e]
            s = jnp.concatenate([s[:, :t] + bias, s[:, t:] + bias], axis=1)
            s_ref[e] = s
            mx_ref[e] = jnp.max(s, axis=0, keepdims=True)

    def tile(j, bufs, shifts, carry):
        s_ref, mx_ref = bufs
        new = [[] for _ in heads]
        for n in range(2 * t // _QS):
            lo = n * _QS
            for e in heads:
                m_old, l_old = (c[:, lo:lo + _QS] for c in carry[e])
                m_new = jnp.maximum(m_old, mx_ref[e, :, lo:lo + _QS] + shifts[e])
                alpha = jnp.exp2(m_old - m_new)
                p = jnp.exp2(s_ref[e, :, lo:lo + _QS] - (m_new - shifts[e]))
                l_new = alpha * l_old + jnp.sum(p, axis=0, keepdims=True)
                acc_sc[e, :, lo:lo + _QS] = (alpha * acc_sc[e, :, lo:lo + _QS]
                                             + jnp.dot(vt_ref[j, e * hw:(e + 1) * hw, :], p.astype(BF16),
                                                       preferred_element_type=F32))
                new[e].append((m_new, l_new))
        return tuple(tuple(jnp.concatenate(parts, axis=1) for parts in zip(*new[e])) for e in heads)

    def off_tile(o, bufs, carry):
        return tile(o, bufs, [-slopes[e] * ((i - o) * t).astype(F32) for e in heads], carry)

    def pair(p, carry):
        o = 2 * p
        scores(o + 1, off_sc, buf_a)
        carry = off_tile(o, buf_b, carry)
        scores(jnp.minimum(o + 2, i - 1), off_sc, buf_b)
        return off_tile(o + 1, buf_a, carry)

    init = tuple((jnp.full((1, 2 * t), NEG, F32), jnp.zeros((1, 2 * t), F32)) for _ in heads)
    acc_sc[...] = jnp.zeros(acc_sc.shape, F32)
    scores(i, diag_sc, buf_a)
    scores(0, off_sc, buf_b)
    carry = tile(i, buf_a, [jnp.zeros((1, 1), F32)] * _HPS, init)
    carry = lax.fori_loop(0, i // 2, pair, carry)
    carry = lax.cond(i % 2 == 1, functools.partial(off_tile, i - 1, buf_b), lambda c: c, carry)

    lv = lam_ref[...]
    lam = (jnp.exp(jnp.sum(lv[0:1] * lv[1:2], axis=-1, keepdims=True))
           - jnp.exp(jnp.sum(lv[2:3] * lv[3:4], axis=-1, keepdims=True)) + lam_init)
    outs = []
    for e in heads:
        o = acc_sc[e] / carry[e][1]
        o = o[:, :t] - lam * o[:, t:]
        outs.append(o * lax.rsqrt(jnp.mean(o * o, axis=0, keepdims=True) + EPS) * (subw_ref[...] * (1.0 - lam_init)))
    o_ref[...] = jnp.concatenate(outs, axis=0).T.astype(BF16)


def _diff_attn(qd, kd, vdt, lam_vecs, subln_w, batch, seq, lam_init):
    n = qd.shape[0]
    t = vdt.shape[2]
    nq = seq // t
    hw = 2 * DIFF_DH
    w = _HPS * hw
    return pl.pallas_call(
        functools.partial(_diff_attn_body, t=t, lam_init=lam_init),
        grid=(batch, DIFF_HEADS // _HPS, nq),
        in_specs=[
            pl.BlockSpec((t, w), lambda b, h, i: (b * nq + i, h)),
            pl.BlockSpec((seq, w), lambda b, h, i: (b, h)),
            pl.BlockSpec((nq, w, t), lambda b, h, i: (b, h, 0)),
            pl.BlockSpec((4, DIFF_DH), lambda b, h, i: (0, 0)),
            pl.BlockSpec((hw, 1), lambda b, h, i: (0, 0)),
        ],
        out_specs=pl.BlockSpec((t, w), lambda b, h, i: (b * nq + i, h)),
        out_shape=jax.ShapeDtypeStruct((n, DIFF_HEADS * hw), BF16),
        scratch_shapes=[pltpu.VMEM((_HPS, t, t), F32), pltpu.VMEM((_HPS, t, t), F32),
                        pltpu.VMEM((_HPS, t, 2 * t), F32), pltpu.VMEM((_HPS, t, 2 * t), F32),
                        pltpu.VMEM((_HPS, 1, 2 * t), F32), pltpu.VMEM((_HPS, 1, 2 * t), F32),
                        pltpu.VMEM((_HPS, hw, 2 * t), F32)],
        compiler_params=_params("parallel", "parallel", "arbitrary"),
        name="diff_attn",
    )(qd, kd, vdt, lam_vecs, subln_w.reshape(hw, 1))


_CT = 256
_CWIN = 3 * _CT
_FW = 1024
_HP = 2 * CH_DH


def _rel_row_index():
    u = (-np.arange(_FW)) % _FW
    u = np.where(u < _CWIN, u, u - _FW)
    rel = CH_LEFT * CHUNK - u
    return np.clip(rel, -REL_CLIP, REL_CLIP) + REL_CLIP


def _chunk_attn_body(q_ref, k0_ref, k1_ref, k2_ref, v0_ref, v1_ref, v2_ref, f_ref, o_ref, bias_sc):
    i = pl.program_id(1)

    @pl.when((pl.program_id(0) == 0) & (i == 0))
    def _():
        c = lax.broadcasted_iota(jnp.int32, (_CWIN, _CT), 0) // CHUNK
        r = lax.broadcasted_iota(jnp.int32, (_CWIN, _CT), 1) // CHUNK
        allowed = (c >= r) & (c <= r + CH_LEFT)
        for h in range(CH_HEADS):
            row = jnp.broadcast_to(f_ref[h:h + 1, :], (_CWIN, _FW))
            toeplitz = pltpu.roll(row, 0, 1, stride=1, stride_axis=0)
            bias_sc[h] = jnp.where(allowed, toeplitz[:, :_CT] * LOG2E, NEG)

    def attend(clamped):
        q = q_ref[...]
        lane = lax.broadcasted_iota(jnp.int32, (_CT, _HP), 1)
        zero = jnp.zeros((_CT, _HP), BF16)
        outs = []
        for hp in range(CH_HEADS // 2):
            sl = slice(hp * _HP, (hp + 1) * _HP)
            qp = q[:, sl]
            qz = jnp.concatenate([jnp.where(lane < CH_DH, qp, zero), jnp.where(lane >= CH_DH, qp, zero)], axis=0)
            kp = jnp.concatenate([k0_ref[:, sl], k1_ref[:, sl], k2_ref[:, sl]], axis=0)
            s = lax.dot_general(kp, qz, _NT, preferred_element_type=F32)
            s = jnp.concatenate([s[:, :_CT] + bias_sc[2 * hp], s[:, _CT:] + bias_sc[2 * hp + 1]], axis=1)
            if clamped:
                pen = [jnp.where(i - 2 + d >= 0, 0.0, NEG).astype(F32) for d in range(2)]
                s = jnp.concatenate([s[:_CT] + pen[0], s[_CT:2 * _CT] + pen[1], s[2 * _CT:]], axis=0)
            m = jnp.max(s, axis=0, keepdims=True)
            p = jnp.exp2(s - m)
            l = jnp.sum(p, axis=0, keepdims=True)
            vt = jnp.concatenate([v0_ref[0, sl, :], v1_ref[0, sl, :], v2_ref[0, sl, :]], axis=1)
            o = jnp.dot(vt, p.astype(BF16), preferred_element_type=F32) / l
            outs += [o[:CH_DH, :_CT], o[CH_DH:, _CT:]]
        o_ref[...] = jnp.concatenate(outs, axis=0).T.astype(BF16)

    @pl.when(i < 2)
    def _():
        attend(True)

    @pl.when(i >= 2)
    def _():
        attend(False)


def _chunk_attn(qc, kc, vct, rel_rows, batch, seq):
    n, w = qc.shape
    nq = seq // _CT
    per_tile = vct.shape[2] // _CT
    block = lambda b, i, d: b * nq + jnp.maximum(i - 2 + d, 0)
    kwin = lambda d: pl.BlockSpec((_CT, w), lambda b, i: (block(b, i, d), 0))
    vwin = lambda d: pl.BlockSpec((1, w, _CT), lambda b, i: (block(b, i, d) // per_tile, 0, block(b, i, d) % per_tile))
    return pl.pallas_call(
        _chunk_attn_body,
        grid=(batch, nq),
        in_specs=[pl.BlockSpec((_CT, w), lambda b, i: (b * nq + i, 0)),
                  kwin(0), kwin(1), kwin(2), vwin(0), vwin(1), vwin(2),
                  pl.BlockSpec((CH_HEADS, _FW), lambda b, i: (0, 0))],
        out_specs=pl.BlockSpec((_CT, w), lambda b, i: (b * nq + i, 0)),
        out_shape=jax.ShapeDtypeStruct((n, w), BF16),
        scratch_shapes=[pltpu.VMEM((CH_HEADS, _CWIN, _CT), F32)],
        compiler_params=_params("arbitrary", "arbitrary"),
        name="chunk_attn",
    )(qc, kc, kc, kc, vct, vct, vct, rel_rows)


def _merge_body(od_ref, oc_ref, g_ref, x_ref, bg_ref, wbd_ref, wbc_ref, wo_ref, n2_ref, wq_ref,
                h_ref, xn_ref, qp_ref):
    d = x_ref.shape[1]
    g = g_ref[...] + bg_ref[...]
    pd = jnp.dot(od_ref[...], wbd_ref[...], preferred_element_type=F32)
    pc = jnp.dot(oc_ref[...], wbc_ref[...], preferred_element_type=F32)
    merged = jax.nn.sigmoid(g[:, :d]) * pd + jax.nn.sigmoid(g[:, d:]) * pc
    h1 = x_ref[...] + jnp.dot(merged.astype(BF16), wo_ref[...], preferred_element_type=F32)
    h_ref[...] = h1
    xn = _rms(h1, n2_ref[...]).astype(BF16)
    xn_ref[...] = xn
    qp_ref[...] = jnp.dot(xn, wq_ref[...], preferred_element_type=F32).astype(BF16)


def _merge(od, oc, gates, x2, b_gate, wbd, wbc, wo, norm2_w, wq, tm):
    n, d = x2.shape
    row = lambda width: pl.BlockSpec((tm, width), lambda i: (i, 0))
    const = lambda shape: pl.BlockSpec(shape, lambda i: (0, 0))
    return pl.pallas_call(
        _merge_body,
        grid=(n // tm,),
        in_specs=[row(od.shape[1]), row(oc.shape[1]), row(2 * d), row(d), const((1, 2 * d)),
                  const(wbd.shape), const(wbc.shape), const(wo.shape), const((1, d)), const(wq.shape)],
        out_specs=[row(d), row(d), row(wq.shape[1])],
        out_shape=[jax.ShapeDtypeStruct((n, d), F32), jax.ShapeDtypeStruct((n, d), BF16),
                   jax.ShapeDtypeStruct((n, wq.shape[1]), BF16)],
        compiler_params=_params("parallel"),
        name="merge",
    )(od, oc, gates, x2, b_gate.reshape(1, 2 * d), wbd, wbc, wo, norm2_w.reshape(1, d), wq)


def _pair_list():
    return [(a, b) for a in range(PEER_TOPK) for b in range(PEER_TOPK) if (a + 1) * (b + 1) <= PEER_TOPK]


def _extract_top(s, steps):
    rows = s.shape[0]
    idx = lax.broadcasted_iota(jnp.int32, s.shape, 0).astype(F32)
    rank = jnp.full(s.shape, float(steps), F32)
    vals = []
    for step in range(steps):
        m = jnp.max(s, axis=0, keepdims=True)
        first = jnp.min(jnp.where(s == m, idx, float(rows)), axis=0, keepdims=True)
        sel = idx == first
        rank = jnp.where(sel, float(step), rank)
        s = jnp.where(sel, -jnp.inf, s)
        vals.append(m)
    return vals, rank


def _extract_top_untied(s, steps):
    rank = jnp.full(s.shape, float(steps), F32)
    vals = []
    for step in range(steps):
        m = jnp.max(s, axis=0, keepdims=True)
        sel = s == m
        rank = jnp.where(sel, float(step), rank)
        s = jnp.where(sel, -jnp.inf, s)
        vals.append(m)
    removed = jnp.sum(jnp.where(rank < float(steps), 1.0, 0.0), axis=0, keepdims=True)
    return vals, rank, jnp.max(jnp.abs(removed - float(steps)))


def _route_outputs(s1, s2, v1, rank1, v2, rank2, r2_ref, bw_ref, ci_ref, ai_ref):
    pairs = _pair_list()
    cand = jnp.concatenate([v1[a] + v2[b] for a, b in pairs], axis=0)
    top = cand[0:1]
    _, crank = _extract_top(cand, PEER_TOPK)
    chosen = crank < float(PEER_TOPK)
    z = jnp.sum(jnp.where(chosen, jnp.exp(cand - top), 0.0), axis=0, keepdims=True)
    ci = jnp.zeros(rank1.shape, F32)
    row = 0
    for a in range(PEER_TOPK):
        nb = PEER_TOPK // (a + 1)
        cnt = jnp.sum(jnp.where(chosen[row:row + nb], 1.0, 0.0), axis=0, keepdims=True)
        ci = jnp.where(rank1 == float(a), cnt, ci)
        row += nb
    r2_ref[0] = rank2.astype(BF16)
    bw_ref[0] = (jnp.exp(s2 - v2[0]) / z).astype(BF16)
    ci_ref[0] = ci
    ai_ref[0] = jnp.exp(s1 - v1[0])


def _route_body(qp_ref, kz_ref, r2_ref, bw_ref, ci_ref, ai_ref):
    nk = PEER_NKEYS
    st = lax.dot_general(kz_ref[0], qp_ref[...], _NT, preferred_element_type=F32)
    s1, s2 = st[:nk], st[nk:]
    outs = (r2_ref, bw_ref, ci_ref, ai_ref)
    v1, rank1, tied1 = _extract_top_untied(s1, PEER_TOPK)
    v2, rank2, tied2 = _extract_top_untied(s2, PEER_TOPK)
    _route_outputs(s1, s2, v1, rank1, v2, rank2, *outs)

    @pl.when(jnp.maximum(tied1, tied2) > 0.0)
    def _():
        _route_outputs(s1, s2, *_extract_top(s1, PEER_TOPK), *_extract_top(s2, PEER_TOPK), *outs)


def _peer_route(qp, kz, tt):
    n, w = qp.shape
    hw = w // PEER_HEADS
    spec = pl.BlockSpec((1, PEER_NKEYS, tt), lambda t, h: (h, 0, t))
    shp = lambda dt: jax.ShapeDtypeStruct((PEER_HEADS, PEER_NKEYS, n), dt)
    return pl.pallas_call(
        _route_body,
        grid=(n // tt, PEER_HEADS),
        in_specs=[pl.BlockSpec((tt, hw), lambda t, h: (t, h)),
                  pl.BlockSpec((1,) + kz.shape[1:], lambda t, h: (h, 0, 0))],
        out_specs=[spec, spec, spec, spec],
        out_shape=[shp(BF16), shp(BF16), shp(F32), shp(F32)],
        compiler_params=_params("parallel", "parallel"),
        name="peer_route",
    )(qp, kz)


_EG = 1024
_OB = 1024


def _row_tile(ref, h, i):
    row = ref[h, pl.ds(i, 1), :]
    return jnp.broadcast_to(row, (_BT, row.shape[1])).astype(BF16)


def _experts_body(xn_ref, u_ref, vt_ref, r2_ref, bw_ref, ci_ref, ai_ref, h_ref, fw_ref, y_ref, acc_sc, *, ec):
    c = pl.program_id(1)
    nk = PEER_NKEYS

    @pl.when(c == 0)
    def _():
        acc_sc[...] = jnp.zeros(acc_sc.shape, F32)

    xn = xn_ref[...]
    tokens = xn.shape[0]
    ngroups = ec // _EG

    def hidden(g):
        return lax.dot_general(u_ref[g * _EG:(g + 1) * _EG, :], xn, _NT, preferred_element_type=F32)

    nxt = hidden(0)
    for g in range(ngroups):
        hid = nxt
        if g + 1 < ngroups:
            nxt = hidden(g + 1)
        acts = []
        for gi in range(_EG // nk):
            i = (c * ec + g * _EG) // nk + gi
            w = None
            for h in range(PEER_HEADS):
                ci = _row_tile(ci_ref, h, i)[None]
                ai = _row_tile(ai_ref, h, i)[None]
                r2 = r2_ref[h].reshape(nk // _BT, _BT, tokens)
                bw = bw_ref[h].reshape(nk // _BT, _BT, tokens)
                term = jnp.where(r2 < ci, bw, jnp.zeros((), BF16)) * ai
                w = term if w is None else w + term
            hg = hid[gi * nk:(gi + 1) * nk]
            gelu = 0.5 * hg * (1.0 + lax.erf(hg * (2.0 ** -0.5)))
            acts.append((gelu * w.reshape(nk, tokens).astype(F32)).astype(BF16))
        act = jnp.concatenate(acts, axis=0)
        for lo in range(0, acc_sc.shape[0], _OB):
            acc_sc[lo:lo + _OB, :] += jnp.dot(vt_ref[lo:lo + _OB, g * _EG:(g + 1) * _EG], act,
                                              preferred_element_type=F32)

    @pl.when(c == pl.num_programs(1) - 1)
    def _():
        hfin = h_ref[...] + acc_sc[...].T
        y_ref[...] = _rms(hfin, fw_ref[...])


def _peer_experts(xn, u_bf16, vt_bf16, r2, bw, ci, ai, h1, final_w, tt, ec):
    n, d = xn.shape
    ne = u_bf16.shape[0]
    gate = pl.BlockSpec((PEER_HEADS, PEER_NKEYS, tt), lambda t, c: (0, 0, t))
    return pl.pallas_call(
        functools.partial(_experts_body, ec=ec),
        grid=(n // tt, ne // ec),
        in_specs=[pl.BlockSpec((tt, d), lambda t, c: (t, 0)),
                  pl.BlockSpec((ec, d), lambda t, c: (c, 0)),
                  pl.BlockSpec((d, ec), lambda t, c: (0, c)),
                  gate, gate, gate, gate,
                  pl.BlockSpec((tt, d), lambda t, c: (t, 0)),
                  pl.BlockSpec((1, d), lambda t, c: (0, 0))],
        out_specs=pl.BlockSpec((tt, d), lambda t, c: (t, 0)),
        out_shape=jax.ShapeDtypeStruct((n, d), F32),
        scratch_shapes=[pltpu.VMEM((d, tt), F32)],
        compiler_params=_params("parallel", "arbitrary"),
        name="peer_experts",
    )(xn, u_bf16, vt_bf16, r2, bw, ci, ai, h1, final_w.reshape(1, d))


def _peer_key_blocks(keys):
    z = jnp.zeros_like(keys[:, 0])
    top = jnp.concatenate([keys[:, 0], z], axis=-1)
    bot = jnp.concatenate([z, keys[:, 1]], axis=-1)
    return jnp.concatenate([top, bot], axis=1).astype(BF16)


def kernel(x, norm1_w, w_in, b_gate, diff_lq1, diff_lk1, diff_lq2, diff_lk2, diff_subln_w, chunk_rel_bias,
           w_branch_diff, w_branch_chunk, w_out, norm2_w, peer_wq, peer_keys, peer_u, peer_v, final_norm_w):
    batch, seq, d = x.shape
    depth = norm1_w.shape[0]
    n = batch * seq
    h = x.reshape(n, d)
    rel_idx = _rel_row_index()
    for l in range(depth):
        lam_init = 0.8 - 0.6 * math.exp(-0.3 * l)
        qd, kd, vdt, qc, kc, vct, gates = _in_proj(h, norm1_w[l], w_in[l].astype(BF16), tm=min(512, seq))
        lam_vecs = jnp.stack([diff_lq1[l], diff_lk1[l], diff_lq2[l], diff_lk2[l]])
        od = _diff_attn(qd, kd, vdt, lam_vecs, diff_subln_w[l], batch, seq, lam_init)
        oc = _chunk_attn(qc, kc, vct, chunk_rel_bias[l][:, rel_idx], batch, seq)
        h1, xn2, qp = _merge(od, oc, gates, h, b_gate[l], w_branch_diff[l].astype(BF16),
                             w_branch_chunk[l].astype(BF16), w_out[l].astype(BF16), norm2_w[l],
                             peer_wq[l].astype(BF16), tm=min(512, n))
        tt = min(512, n)
        r2, bw, ci, ai = _peer_route(qp, _peer_key_blocks(peer_keys[l]), tt)
        last = l == depth - 1
        fw = final_norm_w if last else jnp.ones((d,), F32)
        h = _peer_experts(xn2, peer_u[l].astype(BF16), peer_v[l].T.astype(BF16), r2, bw, ci, ai, h1, fw,
                          tt=tt, ec=2048)
        assert last, "the fused final RMSNorm assumes a single layer"
    return h.reshape(batch, seq, d)
```

```python
import functools
import math

import jax
import jax.numpy as jnp
import numpy as np
from jax import lax
from jax.experimental import pallas as pl
from jax.experimental.pallas import tpu as pltpu

F32 = jnp.float32
BF16 = jnp.bfloat16

EPS = 1e-6
CHUNK = 64
DIFF_HEADS = 8
DIFF_DH = 64
CH_HEADS = 8
CH_DH = 64
CH_LEFT = 8
REL_CLIP = 128
PEER_HEADS = 8
PEER_NKEYS = 128
PEER_TOPK = 16
NEG = -1e30
_BT = 16

VMEM_LIMIT = 56 * 1024 * 1024

_NT = (((1,), (1,)), ((), ()))


def _params(*sem):
    return pltpu.CompilerParams(dimension_semantics=sem, vmem_limit_bytes=VMEM_LIMIT)


def _rms(x, w):
    return x * lax.rsqrt(jnp.mean(x * x, axis=-1, keepdims=True) + EPS) * w


_QKV_WIDTHS = (1024, 1024, 1024, 512, 512, 512)
LOG2E = math.log2(math.e)
_Q_SCALE = (DIFF_DH ** -0.5 * LOG2E, None, None, CH_DH ** -0.5 * LOG2E, None, None)
_TRANSPOSED = (2, 5)


def _inproj_body(x_ref, nw_ref, w_ref, wvt_ref, qd_ref, kd_ref, vdt_ref, qc_ref, kc_ref, vct_ref, g_ref):
    xn = _rms(x_ref[...], nw_ref[...]).astype(BF16)
    col = 0
    outs = (qd_ref, kd_ref, None, qc_ref, kc_ref, None)
    for ref, width, scale in zip(outs, _QKV_WIDTHS, _Q_SCALE):
        if ref is not None:
            acc = jnp.dot(xn, w_ref[:, col:col + width], preferred_element_type=F32)
            if scale is not None:
                acc = acc * scale
            ref[...] = acc.astype(BF16)
        col += width
    g_ref[...] = jnp.dot(xn, w_ref[:, col:], preferred_element_type=F32)
    vt = lax.dot_general(wvt_ref[...], xn, _NT, preferred_element_type=F32).astype(BF16)
    nd = vdt_ref.shape[1]
    vdt_ref[0] = vt[:nd]
    vct_ref[0] = vt[nd:]


def _in_proj(x2, norm_w, w_in_bf16, tm):
    n, d = x2.shape
    cols = w_in_bf16.shape[1]
    starts = np.cumsum((0,) + _QKV_WIDTHS)
    wvt = jnp.concatenate([w_in_bf16[:, starts[k]:starts[k + 1]].T for k in _TRANSPOSED], axis=0)
    row = lambda width: pl.BlockSpec((tm, width), lambda i: (i, 0))
    const = lambda shape: pl.BlockSpec(shape, lambda i: (0, 0))
    out_specs = [row(w) for w in _QKV_WIDTHS] + [row(2 * d)]
    out_shape = [jax.ShapeDtypeStruct((n, w), BF16) for w in _QKV_WIDTHS] + [jax.ShapeDtypeStruct((n, 2 * d), F32)]
    for k in _TRANSPOSED:
        out_specs[k] = pl.BlockSpec((1, _QKV_WIDTHS[k], tm), lambda i: (i, 0, 0))
        out_shape[k] = jax.ShapeDtypeStruct((n // tm, _QKV_WIDTHS[k], tm), BF16)
    return pl.pallas_call(
        _inproj_body,
        grid=(n // tm,),
        in_specs=[row(d), const((1, d)), const((d, cols)), const(wvt.shape)],
        out_specs=out_specs,
        out_shape=out_shape,
        compiler_params=_params("parallel"),
        name="in_proj",
    )(x2, norm_w.reshape(1, d), w_in_bf16, wvt)


_QS = 256
_HPS = 4


def _diff_attn_body(q_ref, k_ref, vt_ref, lam_ref, subw_ref, o_ref, off_sc, diag_sc, sa_sc, sb_sc, ma_sc,
                    mb_sc, acc_sc, *, t, lam_init):
    hp = pl.program_id(1)
    i = pl.program_id(2)
    hw = 2 * DIFF_DH
    heads = range(_HPS)
    slopes = [lax.bitcast_convert_type(jnp.full((1, 1), 126 - (hp * _HPS + e), jnp.int32) << 23, F32) * LOG2E
              for e in heads]

    @pl.when(i == 0)
    def _():
        kpos = lax.broadcasted_iota(jnp.int32, (t, t), 0)
        qpos = lax.broadcasted_iota(jnp.int32, (t, t), 1)
        dist = (qpos - kpos).astype(F32)
        for e in heads:
            off_sc[e] = -slopes[e] * dist
            diag_sc[e] = jnp.where((kpos // CHUNK) <= (qpos // CHUNK), -slopes[e] * jnp.abs(dist), NEG)

    lane = lax.broadcasted_iota(jnp.int32, (t, hw), 1)
    zero = jnp.zeros((t, hw), BF16)
    qz = []
    for e in heads:
        q = q_ref[:, e * hw:(e + 1) * hw]
        qz.append(jnp.concatenate([jnp.where(lane < DIFF_DH, q, zero), jnp.where(lane >= DIFF_DH, q, zero)], axis=0))

    buf_a, buf_b = (sa_sc, ma_sc), (sb_sc, mb_sc)

    def scores(j, bias_sc, bufs):
        s_ref, mx_ref = bufs
        for e in heads:
            kb = k_ref[pl.ds(pl.multiple_of(j * t, t), t), e * hw:(e + 1) * hw]
            s = lax.dot_general(kb, qz[e], _NT, preferred_element_type=F32)
            bias = bias_sc[e]
            s = jnp.concatenate([s[:, :t] + bias, s[:, t:] + bias], axis=1)
            s_ref[e] = s
            mx_ref[e] = jnp.max(s, axis=0, keepdims=True)

    def tile(j, bufs, shifts, carry):
        s_ref, mx_ref = bufs
        new = [[] for _ in heads]
        for n in range(2 * t // _QS):
            lo = n * _QS
            for e in heads:
                m_old, l_old = (c[:, lo:lo + _QS] for c in carry[e])
                m_new = jnp.maximum(m_old, mx_ref[e, :, lo:lo + _QS] + shifts[e])
                alpha = jnp.exp2(m_old - m_new)
                p = jnp.exp2(s_ref[e, :, lo:lo + _QS] - (m_new - shifts[e]))
                l_new = alpha * l_old + jnp.sum(p, axis=0, keepdims=True)
                acc_sc[e, :, lo:lo + _QS] = (alpha * acc_sc[e, :, lo:lo + _QS]
                                             + jnp.dot(vt_ref[j, e * hw:(e + 1) * hw, :], p.astype(BF16),
                                                       preferred_element_type=F32))
                new[e].append((m_new, l_new))
        return tuple(tuple(jnp.concatenate(parts, axis=1) for parts in zip(*new[e])) for e in heads)

    def off_tile(o, bufs, carry):
        return tile(o, bufs, [-slopes[e] * ((i - o) * t).astype(F32) for e in heads], carry)

    def pair(p, carry):
        o = 2 * p
        scores(o + 1, off_sc, buf_a)
        carry = off_tile(o, buf_b, carry)
        scores(jnp.minimum(o + 2, i - 1), off_sc, buf_b)
        return off_tile(o + 1, buf_a, carry)

    init = tuple((jnp.full((1, 2 * t), NEG, F32), jnp.zeros((1, 2 * t), F32)) for _ in heads)
    acc_sc[...] = jnp.zeros(acc_sc.shape, F32)
    scores(i, diag_sc, buf_a)
    scores(0, off_sc, buf_b)
    carry = tile(i, buf_a, [jnp.zeros((1, 1), F32)] * _HPS, init)
    carry = lax.fori_loop(0, i // 2, pair, carry)
    carry = lax.cond(i % 2 == 1, functools.partial(off_tile, i - 1, buf_b), lambda c: c, carry)

    lv = lam_ref[...]
    lam = (jnp.exp(jnp.sum(lv[0:1] * lv[1:2], axis=-1, keepdims=True))
           - jnp.exp(jnp.sum(lv[2:3] * lv[3:4], axis=-1, keepdims=True)) + lam_init)
    outs = []
    for e in heads:
        o = acc_sc[e] / carry[e][1]
        o = o[:, :t] - lam * o[:, t:]
        outs.append(o * lax.rsqrt(jnp.mean(o * o, axis=0, keepdims=True) + EPS) * (subw_ref[...] * (1.0 - lam_init)))
    o_ref[...] = jnp.concatenate(outs, axis=0).T.astype(BF16)


def _diff_attn(qd, kd, vdt, lam_vecs, subln_w, batch, seq, lam_init):
    n = qd.shape[0]
    t = vdt.shape[2]
    nq = seq // t
    hw = 2 * DIFF_DH
    w = _HPS * hw
    return pl.pallas_call(
        functools.partial(_diff_attn_body, t=t, lam_init=lam_init),
        grid=(batch, DIFF_HEADS // _HPS, nq),
        in_specs=[
            pl.BlockSpec((t, w), lambda b, h, i: (b * nq + i, h)),
            pl.BlockSpec((seq, w), lambda b, h, i: (b, h), pipeline_mode=pl.Buffered(1)),
            pl.BlockSpec((nq, w, t), lambda b, h, i: (b, h, 0), pipeline_mode=pl.Buffered(1)),
            pl.BlockSpec((4, DIFF_DH), lambda b, h, i: (0, 0)),
            pl.BlockSpec((hw, 1), lambda b, h, i: (0, 0)),
        ],
        out_specs=pl.BlockSpec((t, w), lambda b, h, i: (b * nq + i, h)),
        out_shape=jax.ShapeDtypeStruct((n, DIFF_HEADS * hw), BF16),
        scratch_shapes=[pltpu.VMEM((_HPS, t, t), F32), pltpu.VMEM((_HPS, t, t), F32),
                        pltpu.VMEM((_HPS, t, 2 * t), F32), pltpu.VMEM((_HPS, t, 2 * t), F32),
                        pltpu.VMEM((_HPS, 1, 2 * t), F32), pltpu.VMEM((_HPS, 1, 2 * t), F32),
                        pltpu.VMEM((_HPS, hw, 2 * t), F32)],
        compiler_params=_params("parallel", "parallel", "arbitrary"),
        name="diff_attn",
    )(qd, kd, vdt, lam_vecs, subln_w.reshape(hw, 1))


_CT = 256
_CWIN = 3 * _CT
_FW = 1024
_HP = 2 * CH_DH


def _rel_row_index():
    u = (-np.arange(_FW)) % _FW
    u = np.where(u < _CWIN, u, u - _FW)
    rel = CH_LEFT * CHUNK - u
    return np.clip(rel, -REL_CLIP, REL_CLIP) + REL_CLIP


def _chunk_attn_body(q_ref, k0_ref, k1_ref, k2_ref, v0_ref, v1_ref, v2_ref, f_ref, o_ref, bias_sc):
    i = pl.program_id(1)

    @pl.when((pl.program_id(0) == 0) & (i == 0))
    def _():
        c = lax.broadcasted_iota(jnp.int32, (_CWIN, _CT), 0) // CHUNK
        r = lax.broadcasted_iota(jnp.int32, (_CWIN, _CT), 1) // CHUNK
        allowed = (c >= r) & (c <= r + CH_LEFT)
        for h in range(CH_HEADS):
            row = jnp.broadcast_to(f_ref[h:h + 1, :], (_CWIN, _FW))
            toeplitz = pltpu.roll(row, 0, 1, stride=1, stride_axis=0)
            bias_sc[h] = jnp.where(allowed, toeplitz[:, :_CT] * LOG2E, NEG)

    def attend(clamped):
        q = q_ref[...]
        lane = lax.broadcasted_iota(jnp.int32, (_CT, _HP), 1)
        zero = jnp.zeros((_CT, _HP), BF16)
        outs = []
        raw = []
        for hp in range(CH_HEADS // 2):
            sl = slice(hp * _HP, (hp + 1) * _HP)
            qp = q[:, sl]
            qz = jnp.concatenate([jnp.where(lane < CH_DH, qp, zero), jnp.where(lane >= CH_DH, qp, zero)], axis=0)
            kp = jnp.concatenate([k0_ref[:, sl], k1_ref[:, sl], k2_ref[:, sl]], axis=0)
            raw.append(lax.dot_general(kp, qz, _NT, preferred_element_type=F32))
        for hp in range(CH_HEADS // 2):
            sl = slice(hp * _HP, (hp + 1) * _HP)
            s = raw[hp]
            s = jnp.concatenate([s[:, :_CT] + bias_sc[2 * hp], s[:, _CT:] + bias_sc[2 * hp + 1]], axis=1)
            if clamped:
                pen = [jnp.where(i - 2 + d >= 0, 0.0, NEG).astype(F32) for d in range(2)]
                s = jnp.concatenate([s[:_CT] + pen[0], s[_CT:2 * _CT] + pen[1], s[2 * _CT:]], axis=0)
            m = jnp.max(s, axis=0, keepdims=True)
            p = jnp.exp2(s - m)
            l = jnp.sum(p, axis=0, keepdims=True)
            vt = jnp.concatenate([v0_ref[0, sl, :], v1_ref[0, sl, :], v2_ref[0, sl, :]], axis=1)
            o = jnp.dot(vt, p.astype(BF16), preferred_element_type=F32) / l
            outs += [o[:CH_DH, :_CT], o[CH_DH:, _CT:]]
        o_ref[...] = jnp.concatenate(outs, axis=0).T.astype(BF16)

    @pl.when(i < 2)
    def _():
        attend(True)

    @pl.when(i >= 2)
    def _():
        attend(False)


def _chunk_attn(qc, kc, vct, rel_rows, batch, seq):
    n, w = qc.shape
    nq = seq // _CT
    per_tile = vct.shape[2] // _CT
    block = lambda b, i, d: b * nq + jnp.maximum(i - 2 + d, 0)
    kwin = lambda d: pl.BlockSpec((_CT, w), lambda b, i: (block(b, i, d), 0))
    vwin = lambda d: pl.BlockSpec((1, w, _CT), lambda b, i: (block(b, i, d) // per_tile, 0, block(b, i, d) % per_tile))
    return pl.pallas_call(
        _chunk_attn_body,
        grid=(batch, nq),
        in_specs=[pl.BlockSpec((_CT, w), lambda b, i: (b * nq + i, 0)),
                  kwin(0), kwin(1), kwin(2), vwin(0), vwin(1), vwin(2),
                  pl.BlockSpec((CH_HEADS, _FW), lambda b, i: (0, 0))],
        out_specs=pl.BlockSpec((_CT, w), lambda b, i: (b * nq + i, 0)),
        out_shape=jax.ShapeDtypeStruct((n, w), BF16),
        scratch_shapes=[pltpu.VMEM((CH_HEADS, _CWIN, _CT), F32)],
        compiler_params=_params("arbitrary", "arbitrary"),
        name="chunk_attn",
    )(qc, kc, kc, kc, vct, vct, vct, rel_rows)


def _merge_body(od_ref, oc_ref, g_ref, x_ref, bg_ref, wbd_ref, wbc_ref, wo_ref, n2_ref, wq_ref,
                h_ref, xn_ref, qp_ref):
    d = x_ref.shape[1]
    g = g_ref[...] + bg_ref[...]
    pd = jnp.dot(od_ref[...], wbd_ref[...], preferred_element_type=F32)
    pc = jnp.dot(oc_ref[...], wbc_ref[...], preferred_element_type=F32)
    merged = jax.nn.sigmoid(g[:, :d]) * pd + jax.nn.sigmoid(g[:, d:]) * pc
    h1 = x_ref[...] + jnp.dot(merged.astype(BF16), wo_ref[...], preferred_element_type=F32)
    h_ref[...] = h1
    xn = _rms(h1, n2_ref[...]).astype(BF16)
    xn_ref[...] = xn
    qp_ref[...] = jnp.dot(xn, wq_ref[...], preferred_element_type=F32).astype(BF16)


def _merge(od, oc, gates, x2, b_gate, wbd, wbc, wo, norm2_w, wq, tm):
    n, d = x2.shape
    row = lambda width: pl.BlockSpec((tm, width), lambda i: (i, 0))
    const = lambda shape: pl.BlockSpec(shape, lambda i: (0, 0))
    return pl.pallas_call(
        _merge_body,
        grid=(n // tm,),
        in_specs=[row(od.shape[1]), row(oc.shape[1]), row(2 * d), row(d), const((1, 2 * d)),
                  const(wbd.shape), const(wbc.shape), const(wo.shape), const((1, d)), const(wq.shape)],
        out_specs=[row(d), row(d), row(wq.shape[1])],
        out_shape=[jax.ShapeDtypeStruct((n, d), F32), jax.ShapeDtypeStruct((n, d), BF16),
                   jax.ShapeDtypeStruct((n, wq.shape[1]), BF16)],
        compiler_params=_params("parallel"),
        name="merge",
    )(od, oc, gates, x2, b_gate.reshape(1, 2 * d), wbd, wbc, wo, norm2_w.reshape(1, d), wq)


def _pair_list():
    return [(a, b) for a in range(PEER_TOPK) for b in range(PEER_TOPK) if (a + 1) * (b + 1) <= PEER_TOPK]


def _extract_top(s, steps):
    rows = s.shape[0]
    idx = lax.broadcasted_iota(jnp.int32, s.shape, 0).astype(F32)
    rank = jnp.full(s.shape, float(steps), F32)
    vals = []
    for step in range(steps):
        m = jnp.max(s, axis=0, keepdims=True)
        first = jnp.min(jnp.where(s == m, idx, float(rows)), axis=0, keepdims=True)
        sel = idx == first
        rank = jnp.where(sel, float(step), rank)
        s = jnp.where(sel, -jnp.inf, s)
        vals.append(m)
    return vals, rank


def _extract_top_untied(s, steps):
    rank = jnp.full(s.shape, float(steps), F32)
    vals = []
    for step in range(steps):
        m = jnp.max(s, axis=0, keepdims=True)
        sel = s == m
        rank = jnp.where(sel, float(step), rank)
        s = jnp.where(sel, -jnp.inf, s)
        vals.append(m)
    removed = jnp.sum(jnp.where(rank < float(steps), 1.0, 0.0), axis=0, keepdims=True)
    return vals, rank, jnp.max(jnp.abs(removed - float(steps)))


def _route_outputs(s1, s2, v1, rank1, v2, rank2, r2_ref, bw_ref, ci_ref, ai_ref):
    pairs = _pair_list()
    cand = jnp.concatenate([v1[a] + v2[b] for a, b in pairs], axis=0)
    top = cand[0:1]
    _, crank = _extract_top(cand, PEER_TOPK)
    chosen = crank < float(PEER_TOPK)
    z = jnp.sum(jnp.where(chosen, jnp.exp(cand - top), 0.0), axis=0, keepdims=True)
    ci = jnp.zeros(rank1.shape, F32)
    row = 0
    for a in range(PEER_TOPK):
        nb = PEER_TOPK // (a + 1)
        cnt = jnp.sum(jnp.where(chosen[row:row + nb], 1.0, 0.0), axis=0, keepdims=True)
        ci = jnp.where(rank1 == float(a), cnt, ci)
        row += nb
    r2_ref[0] = rank2.astype(BF16)
    bw_ref[0] = (jnp.exp(s2 - v2[0]) / z).astype(BF16)
    ci_ref[0] = ci
    ai_ref[0] = jnp.exp(s1 - v1[0])


def _route_body(qp_ref, kz_ref, r2_ref, bw_ref, ci_ref, ai_ref):
    nk = PEER_NKEYS
    st = lax.dot_general(kz_ref[0], qp_ref[...], _NT, preferred_element_type=F32)
    s1, s2 = st[:nk], st[nk:]
    outs = (r2_ref, bw_ref, ci_ref, ai_ref)
    v1, rank1, tied1 = _extract_top_untied(s1, PEER_TOPK)
    v2, rank2, tied2 = _extract_top_untied(s2, PEER_TOPK)
    _route_outputs(s1, s2, v1, rank1, v2, rank2, *outs)

    @pl.when(jnp.maximum(tied1, tied2) > 0.0)
    def _():
        _route_outputs(s1, s2, *_extract_top(s1, PEER_TOPK), *_extract_top(s2, PEER_TOPK), *outs)


def _peer_route(qp, kz, tt):
    n, w = qp.shape
    hw = w // PEER_HEADS
    spec = pl.BlockSpec((1, PEER_NKEYS, tt), lambda t, h: (h, 0, t))
    shp = lambda dt: jax.ShapeDtypeStruct((PEER_HEADS, PEER_NKEYS, n), dt)
    return pl.pallas_call(
        _route_body,
        grid=(n // tt, PEER_HEADS),
        in_specs=[pl.BlockSpec((tt, hw), lambda t, h: (t, h)),
                  pl.BlockSpec((1,) + kz.shape[1:], lambda t, h: (h, 0, 0))],
        out_specs=[spec, spec, spec, spec],
        out_shape=[shp(BF16), shp(BF16), shp(F32), shp(F32)],
        compiler_params=_params("parallel", "parallel"),
        name="peer_route",
    )(qp, kz)


_EG = 1024
_OB = 1024


def _row_tile(ref, h, i):
    row = ref[h, pl.ds(i, 1), :]
    return jnp.broadcast_to(row, (_BT, row.shape[1])).astype(BF16)


def _experts_body(xn_ref, u_ref, vt_ref, r2_ref, bw_ref, ci_ref, ai_ref, h_ref, fw_ref, y_ref, acc_sc, *, ec):
    c = pl.program_id(1)
    nk = PEER_NKEYS

    @pl.when(c == 0)
    def _():
        acc_sc[...] = jnp.zeros(acc_sc.shape, F32)

    xn = xn_ref[...]
    tokens = xn.shape[0]
    ngroups = ec // _EG

    def hidden(g):
        return lax.dot_general(u_ref[g * _EG:(g + 1) * _EG, :], xn, _NT, preferred_element_type=F32)

    nxt = hidden(0)
    for g in range(ngroups):
        hid = nxt
        if g + 1 < ngroups:
            nxt = hidden(g + 1)
        acts = []
        for gi in range(_EG // nk):
            i = (c * ec + g * _EG) // nk + gi
            w = None
            for h in range(PEER_HEADS):
                ci = _row_tile(ci_ref, h, i)[None]
                ai = _row_tile(ai_ref, h, i)[None]
                r2 = r2_ref[h].reshape(nk // _BT, _BT, tokens)
                bw = bw_ref[h].reshape(nk // _BT, _BT, tokens)
                term = jnp.where(r2 < ci, bw, jnp.zeros((), BF16)) * ai
                w = term if w is None else w + term
            hg = hid[gi * nk:(gi + 1) * nk]
            gelu = 0.5 * hg * (1.0 + lax.erf(hg * (2.0 ** -0.5)))
            acts.append((gelu * w.reshape(nk, tokens).astype(F32)).astype(BF16))
        act = jnp.concatenate(acts, axis=0)
        for lo in range(0, acc_sc.shape[0], _OB):
            acc_sc[lo:lo + _OB, :] += jnp.dot(vt_ref[lo:lo + _OB, g * _EG:(g + 1) * _EG], act,
                                              preferred_element_type=F32)

    @pl.when(c == pl.num_programs(1) - 1)
    def _():
        hfin = h_ref[...] + acc_sc[...].T
        y_ref[...] = _rms(hfin, fw_ref[...])


def _peer_experts(xn, u_bf16, vt_bf16, r2, bw, ci, ai, h1, final_w, tt, ec):
    n, d = xn.shape
    ne = u_bf16.shape[0]
    gate = pl.BlockSpec((PEER_HEADS, PEER_NKEYS, tt), lambda t, c: (0, 0, t))
    return pl.pallas_call(
        functools.partial(_experts_body, ec=ec),
        grid=(n // tt, ne // ec),
        in_specs=[pl.BlockSpec((tt, d), lambda t, c: (t, 0)),
                  pl.BlockSpec((ec, d), lambda t, c: (c, 0)),
                  pl.BlockSpec((d, ec), lambda t, c: (0, c)),
                  gate, gate, gate, gate,
                  pl.BlockSpec((tt, d), lambda t, c: (t, 0)),
                  pl.BlockSpec((1, d), lambda t, c: (0, 0))],
        out_specs=pl.BlockSpec((tt, d), lambda t, c: (t, 0)),
        out_shape=jax.ShapeDtypeStruct((n, d), F32),
        scratch_shapes=[pltpu.VMEM((d, tt), F32)],
        compiler_params=_params("parallel", "arbitrary"),
        name="peer_experts",
    )(xn, u_bf16, vt_bf16, r2, bw, ci, ai, h1, final_w.reshape(1, d))


def _peer_key_blocks(keys):
    z = jnp.zeros_like(keys[:, 0])
    top = jnp.concatenate([keys[:, 0], z], axis=-1)
    bot = jnp.concatenate([z, keys[:, 1]], axis=-1)
    return jnp.concatenate([top, bot], axis=1).astype(BF16)


def kernel(x, norm1_w, w_in, b_gate, diff_lq1, diff_lk1, diff_lq2, diff_lk2, diff_subln_w, chunk_rel_bias,
           w_branch_diff, w_branch_chunk, w_out, norm2_w, peer_wq, peer_keys, peer_u, peer_v, final_norm_w):
    batch, seq, d = x.shape
    depth = norm1_w.shape[0]
    n = batch * seq
    h = x.reshape(n, d)
    rel_idx = _rel_row_index()
    for l in range(depth):
        lam_init = 0.8 - 0.6 * math.exp(-0.3 * l)
        qd, kd, vdt, qc, kc, vct, gates = _in_proj(h, norm1_w[l], w_in[l].astype(BF16), tm=min(512, seq))
        lam_vecs = jnp.stack([diff_lq1[l], diff_lk1[l], diff_lq2[l], diff_lk2[l]])
        od = _diff_attn(qd, kd, vdt, lam_vecs, diff_subln_w[l], batch, seq, lam_init)
        oc = _chunk_attn(qc, kc, vct, chunk_rel_bias[l][:, rel_idx], batch, seq)
        h1, xn2, qp = _merge(od, oc, gates, h, b_gate[l], w_branch_diff[l].astype(BF16),
                             w_branch_chunk[l].astype(BF16), w_out[l].astype(BF16), norm2_w[l],
                             peer_wq[l].astype(BF16), tm=min(512, n))
        tt = min(512, n)
        r2, bw, ci, ai = _peer_route(qp, _peer_key_blocks(peer_keys[l]), tt)
        last = l == depth - 1
        fw = final_norm_w if last else jnp.ones((d,), F32)
        h = _peer_experts(xn2, peer_u[l].astype(BF16), peer_v[l].T.astype(BF16), r2, bw, ci, ai, h1, fw,
                          tt=tt, ec=2048)
        assert last, "the fused final RMSNorm assumes a single layer"
    return h.reshape(batch, seq, d)
```
